```python
import math
import jax, jax.numpy as jnp
from jax import lax
import numpy as np

D_MODEL = 1024
BATCH = 2
SEQ = 16384
DEPTH = 4
DEC_BATCH = 16
DEC_SEQ = 4096
PAST_LEN = 128

N_MIXERS = 3
N_A = (DEPTH + 2) // 3
N_B = (DEPTH + 1) // 3
N_C = DEPTH // 3
DA_HEADS = 8
DA_DK = 64
DA_DV = 2 * DA_DK
Q_BLOCK = 128
WA_HEADS = 16
WA_KV_HEADS = 4
WA_GROUP = WA_HEADS // WA_KV_HEADS
WA_DH = 64
WINDOW = 128
BLOCK = 128
D_RNN = 1536
RG_BLOCKS = 16
RG_BW = D_RNN // RG_BLOCKS
RG_C = 8.0
RG_CONV_W = 4
RG_CONV_LEFT = 2
D_FF = 2816
FFN_CONV_W = 3
FFN_CONV_LEFT = 1
ROPE_THETA = 500000.0
ROT_FRAC = 4
EPS = 1e-6

kernel_name = "hybrid_bidir_diffattn_swa_rglru_convffn"


def rmsnorm(x, g):
    xf = x.astype(jnp.float32)
    y = xf * lax.rsqrt(jnp.mean(xf * xf, axis=-1, keepdims=True) + EPS)
    return (y * g.astype(jnp.float32)).astype(x.dtype)


def rope_partial(x):
    S, dh = x.shape[1], x.shape[-1]
    rot = dh // ROT_FRAC
    half = rot // 2
    inv = ROPE_THETA ** (-jnp.arange(half, dtype=jnp.float32) * 2.0 / rot)
    ang = jnp.arange(S, dtype=jnp.float32)[:, None] * inv[None, :]
    bshape = (1, S) + (1,) * (x.ndim - 3) + (half,)
    cos = jnp.cos(ang).reshape(bshape)
    sin = jnp.sin(ang).reshape(bshape)
    xf = x.astype(jnp.float32)
    x1, x2 = xf[..., :half], xf[..., half:rot]
    out = jnp.concatenate([x1 * cos - x2 * sin, x2 * cos + x1 * sin, xf[..., rot:]], axis=-1)
    return out.astype(x.dtype)


def dwconv(x, w, b, left):
    K, C = w.shape
    y = lax.conv_general_dilated(x, w.reshape(K, 1, C).astype(x.dtype), window_strides=(1,),
                                 padding=[(left, K - 1 - left)],
                                 dimension_numbers=('NWC', 'WIO', 'NWC'),
                                 feature_group_count=C)
    return y + b.astype(x.dtype)


def diff_lambda_init(layer_idx):
    return 0.8 - 0.6 * math.exp(-0.3 * layer_idx)


def diff_attention(h, w_qkv, q_g, k_g, lq1, lk1, lq2, lk2, sub_g, w_o, lambda_init):
    B, S, _ = h.shape
    nq = S // Q_BLOCK
    qk_w = DA_HEADS * 2 * DA_DK
    qkv = h @ w_qkv
    q = qkv[..., :qk_w].reshape(B, S, DA_HEADS, 2, DA_DK)
    k = qkv[..., qk_w:2 * qk_w].reshape(B, S, DA_HEADS, 2, DA_DK)
    v = qkv[..., 2 * qk_w:].reshape(B, S, DA_HEADS, DA_DV)
    q = rope_partial(rmsnorm(q, q_g)) * (DA_DK ** -0.5)
    k = rope_partial(rmsnorm(k, k_g))
    f32 = jnp.float32
    lam = (jnp.exp(jnp.sum(lq1.astype(f32) * lk1.astype(f32)))
           - jnp.exp(jnp.sum(lq2.astype(f32) * lk2.astype(f32))) + lambda_init)
    q_blocks = jnp.moveaxis(q.reshape(B, nq, Q_BLOCK, DA_HEADS, 2, DA_DK), 1, 0)

    def attend(qb):
        s = jnp.einsum('bqhmd,bkhmd->bhmqk', qb, k, preferred_element_type=f32)
        p = jax.nn.softmax(s, axis=-1)
        w = p[:, :, 0] - lam * p[:, :, 1]
        return jnp.einsum('bhqk,bkhd->bqhd', w.astype(v.dtype), v)

    o = lax.map(attend, q_blocks)
    o = jnp.moveaxis(o, 0, 1).reshape(B, S, DA_HEADS, DA_DV)
    o = rmsnorm(o, sub_g) * (1.0 - lambda_init)
    return o.reshape(B, S, DA_HEADS * DA_DV) @ w_o


def window_attention(h, w_qkv, q_g, k_g, sink, w_o):
    B, S, _ = h.shape
    nb = S // BLOCK
    q_w = WA_HEADS * WA_DH
    kv_w = WA_KV_HEADS * WA_DH
    qkv = h @ w_qkv
    q = qkv[..., :q_w].reshape(B, S, WA_HEADS, WA_DH)
    k = qkv[..., q_w:q_w + kv_w].reshape(B, S, WA_KV_HEADS, WA_DH)
    v = qkv[..., q_w + kv_w:].reshape(B, S, WA_KV_HEADS, WA_DH)
    q = rope_partial(rmsnorm(q, q_g)) * (WA_DH ** -0.5)
    k = rope_partial(rmsnorm(k, k_g))
    pad = ((0, 0), (BLOCK, BLOCK), (0, 0), (0, 0))
    kp = jnp.pad(k, pad)
    vp = jnp.pad(v, pad)
    q_blocks = jnp.moveaxis(q.reshape(B, nb, BLOCK, WA_KV_HEADS, WA_GROUP, WA_DH), 1, 0)
    sink_b = sink.astype(jnp.float32).reshape(1, WA_KV_HEADS, WA_GROUP, 1, 1)
    q_off = jnp.arange(BLOCK)
    k_off = jnp.arange(3 * BLOCK)

    def attend(args):
        qb, j = args
        kb = lax.dynamic_slice_in_dim(kp, j * BLOCK, 3 * BLOCK, axis=1)
        vb = lax.dynamic_slice_in_dim(vp, j * BLOCK, 3 * BLOCK, axis=1)
        qpos = j * BLOCK + q_off
        kpos = (j - 1) * BLOCK + k_off
        valid = ((kpos >= 0) & (kpos < S))[None, :] & (jnp.abs(qpos[:, None] - kpos[None, :]) <= WINDOW)
        s = jnp.einsum('bqkgd,bskd->bkgqs', qb, kb, preferred_element_type=jnp.float32)
        s = jnp.where(valid, s, -jnp.inf)
        m = jnp.maximum(jnp.max(s, axis=-1, keepdims=True), sink_b)
        p = jnp.exp(s - m)
        p = p / (jnp.sum(p, axis=-1, keepdims=True) + jnp.exp(sink_b - m))
        return jnp.einsum('bkgqs,bskd->bqkgd', p.astype(vb.dtype), vb)

    o = lax.map(attend, (q_blocks, jnp.arange(nb)))
    o = jnp.moveaxis(o, 0, 1).reshape(B, S, WA_HEADS * WA_DH)
    return o @ w_o


def _lru_combine(left, right):
    a_l, b_l = left
    a_r, b_r = right
    return a_l * a_r, a_r * b_l + b_r


def rglru_mixer(h, w_in, conv_w, conv_b, gate_w, gate_b, lam, w_out):
    B, S, _ = h.shape
    f32 = jnp.float32
    gu = h @ w_in
    gate, u = gu[..., :D_RNN], gu[..., D_RNN:]
    u = dwconv(u, conv_w, conv_b, RG_CONV_LEFT)
    uf = u.astype(f32)
    ub = uf.reshape(B, S, RG_BLOCKS, RG_BW)

    def scan_dir(d, reverse):
        ri = jnp.einsum('bsnc,nce->bsne', ub, gate_w[d].astype(f32)) + gate_b[d].astype(f32)
        r = jax.nn.sigmoid(ri[..., :RG_BW]).reshape(B, S, D_RNN)
        i_g = jax.nn.sigmoid(ri[..., RG_BW:]).reshape(B, S, D_RNN)
        log_a = -RG_C * r * jax.nn.softplus(-lam[d].astype(f32))
        a = jnp.exp(log_a)
        bx = jnp.sqrt(-jnp.expm1(2.0 * log_a)) * (i_g * uf)
        _, hs = lax.associative_scan(_lru_combine, (a, bx), axis=1, reverse=reverse)
        return hs

    y = (scan_dir(0, False) + scan_dir(1, True)) * jax.nn.gelu(gate.astype(f32))
    return y.astype(h.dtype) @ w_out


def conv_ffn(h, w_up, conv_w, conv_b, w_down):
    gu = h @ w_up
    g, u = gu[..., :D_FF], gu[..., D_FF:]
    g = dwconv(g, conv_w, conv_b, FFN_CONV_LEFT)
    return (jax.nn.silu(g) * u) @ w_down


def encoder_trunk(x, p):
    for i in range(DEPTH):
        kind, j = i % N_MIXERS, i // N_MIXERS
        h = rmsnorm(x, p['norm_mix'][i])
        if kind == 0:
            m = diff_attention(h, p['da_w_qkv'][j], p['da_q_norm'][j], p['da_k_norm'][j],
                               p['da_lambda_q1'][j], p['da_lambda_k1'][j],
                               p['da_lambda_q2'][j], p['da_lambda_k2'][j],
                               p['da_sub_norm'][j], p['da_w_o'][j], diff_lambda_init(i))
        elif kind == 1:
            m = window_attention(h, p['wa_w_qkv'][j], p['wa_q_norm'][j], p['wa_k_norm'][j],
                                 p['wa_sink'][j], p['wa_w_o'][j])
        else:
            m = rglru_mixer(h, p['rg_w_in'][j], p['rg_conv_w'][j], p['rg_conv_b'][j],
                            p['rg_gate_w'][j], p['rg_gate_b'][j], p['rg_lambda'][j], p['rg_w_out'][j])
        x = x + m
        x = x + conv_ffn(rmsnorm(x, p['norm_ffn'][i]), p['ffn_w_up'][i], p['ffn_conv_w'][i],
                         p['ffn_conv_b'][i], p['ffn_w_down'][i])
    return x


def setup_inputs(seed: int = 0) -> dict:
    key = jax.random.key(seed)
    k = jax.random.split(key, 32)
    f32 = jnp.float32

    def nrm(kk, shape, scale):
        return jax.random.normal(kk, shape, f32) * scale

    def gain(kk, shape):
        return 1.0 + nrm(kk, shape, 0.05)

    D, F = D_MODEL, D_FF
    da_qkv = 2 * DA_HEADS * 2 * DA_DK + DA_HEADS * DA_DV
    wa_qkv = (WA_HEADS + 2 * WA_KV_HEADS) * WA_DH
    u = jax.random.uniform(k[27], (N_C, 2, D_RNN), f32, minval=0.9, maxval=0.999)
    s = u ** (1.0 / RG_C)
    rg_lambda = jnp.log(s) - jnp.log1p(-s)
    return {
        'x_prompt': nrm(k[0], (BATCH, SEQ, D), 1.0),
        'x_sample': nrm(k[1], (DEC_BATCH, DEC_SEQ, D), 1.0),
        'norm_mix': gain(k[2], (DEPTH, D)),
        'norm_ffn': gain(k[3], (DEPTH, D)),
        'ffn_w_up': nrm(k[4], (DEPTH, D, 2 * F), D ** -0.5),
        'ffn_conv_w': nrm(k[5], (DEPTH, FFN_CONV_W, F), FFN_CONV_W ** -0.5),
        'ffn_conv_b': nrm(k[6], (DEPTH, F), 0.02),
        'ffn_w_down': nrm(k[7], (DEPTH, F, D), F ** -0.5),
        'da_w_qkv': nrm(k[8], (N_A, D, da_qkv), D ** -0.5),
        'da_q_norm': gain(k[9], (N_A, DA_DK)),
        'da_k_norm': gain(k[10], (N_A, DA_DK)),
        'da_lambda_q1': nrm(k[11], (N_A, DA_DK), 0.1),
        'da_lambda_k1': nrm(k[12], (N_A, DA_DK), 0.1),
        'da_lambda_q2': nrm(k[13], (N_A, DA_DK), 0.1),
        'da_lambda_k2': nrm(k[14], (N_A, DA_DK), 0.1),
        'da_sub_norm': gain(k[15], (N_A, DA_DV)),
        'da_w_o': nrm(k[16], (N_A, DA_HEADS * DA_DV, D), (DA_HEADS * DA_DV) ** -0.5),
        'wa_w_qkv': nrm(k[17], (N_B, D, wa_qkv), D ** -0.5),
        'wa_q_norm': gain(k[18], (N_B, WA_DH)),
        'wa_k_norm': gain(k[19], (N_B, WA_DH)),
        'wa_sink': nrm(k[20], (N_B, WA_HEADS), 0.5),
        'wa_w_o': nrm(k[21], (N_B, WA_HEADS * WA_DH, D), (WA_HEADS * WA_DH) ** -0.5),
        'rg_w_in': nrm(k[22], (N_C, D, 2 * D_RNN), D ** -0.5),
        'rg_conv_w': nrm(k[23], (N_C, RG_CONV_W, D_RNN), RG_CONV_W ** -0.5),
        'rg_conv_b': nrm(k[24], (N_C, D_RNN), 0.02),
        'rg_gate_w': nrm(k[25], (N_C, 2, RG_BLOCKS, RG_BW, 2 * RG_BW), RG_BW ** -0.5),
        'rg_gate_b': nrm(k[26], (N_C, 2, RG_BLOCKS, 2 * RG_BW), 0.02),
        'rg_lambda': rg_lambda,
        'rg_w_out': nrm(k[28], (N_C, D_RNN, D), D_RNN ** -0.5),
    }


def reference(x_prompt, x_sample, norm_mix, norm_ffn, ffn_w_up, ffn_conv_w, ffn_conv_b, ffn_w_down,
              da_w_qkv, da_q_norm, da_k_norm, da_lambda_q1, da_lambda_k1, da_lambda_q2, da_lambda_k2,
              da_sub_norm, da_w_o, wa_w_qkv, wa_q_norm, wa_k_norm, wa_sink, wa_w_o,
              rg_w_in, rg_conv_w, rg_conv_b, rg_gate_w, rg_gate_b, rg_lambda, rg_w_out):
    p = dict(norm_mix=norm_mix, norm_ffn=norm_ffn, ffn_w_up=ffn_w_up, ffn_conv_w=ffn_conv_w,
             ffn_conv_b=ffn_conv_b, ffn_w_down=ffn_w_down,
             da_w_qkv=da_w_qkv, da_q_norm=da_q_norm, da_k_norm=da_k_norm,
             da_lambda_q1=da_lambda_q1, da_lambda_k1=da_lambda_k1,
             da_lambda_q2=da_lambda_q2, da_lambda_k2=da_lambda_k2,
             da_sub_norm=da_sub_norm, da_w_o=da_w_o,
             wa_w_qkv=wa_w_qkv, wa_q_norm=wa_q_norm, wa_k_norm=wa_k_norm, wa_sink=wa_sink, wa_w_o=wa_w_o,
             rg_w_in=rg_w_in, rg_conv_w=rg_conv_w, rg_conv_b=rg_conv_b, rg_gate_w=rg_gate_w,
             rg_gate_b=rg_gate_b, rg_lambda=rg_lambda, rg_w_out=rg_w_out)
    y_prompt = encoder_trunk(x_prompt, p)
    y_sample = encoder_trunk(x_sample, p)
    return (y_prompt, y_sample)
```

```python
import functools
import math

import jax
import jax.numpy as jnp
from jax import lax
from jax.experimental import pallas as pl
from jax.experimental.pallas import tpu as pltpu

F32 = jnp.float32
BF16 = jnp.bfloat16

DA_HEADS = 8
DA_DK = 64
WA_HEADS = 16
WA_KV_HEADS = 4
WA_DH = 64
WINDOW = 128
RG_BLOCKS = 16
RG_C = 8.0
ROPE_THETA = 500000.0
ROT_FRAC = 4
EPS = 1e-6
N_MIXERS = 3

LANES = 128
MXU_DIM = 256
HALO = 16
VMEM_LIMIT = 56 * 1024 * 1024


def _cparams(sem):
    return pltpu.CompilerParams(dimension_semantics=sem, vmem_limit_bytes=VMEM_LIMIT)


def _rmsnorm_bf16(x, g):
    return (x * lax.rsqrt(jnp.mean(x * x, axis=-1, keepdims=True) + EPS) * g).astype(BF16)


def _resident(shape):
    nd = len(shape)
    return pl.BlockSpec(shape, lambda *_: (0,) * nd)


def _proj_kernel(x_ref, gin_ref, w_ref, geff_ref, cos_ref, sin_ref, o_ref, *, n_prep, cn):
    tm = x_ref.shape[1]
    n_out = w_ref.shape[1]
    h = _rmsnorm_bf16(x_ref[0], gin_ref[...])
    lane = lax.broadcasted_iota(jnp.int32, (tm, LANES), 1)
    lo = lane < 64
    first = (lane & 63) < 8
    cos = cos_ref[...]
    sin = sin_ref[...]
    for j in range(n_out // cn):
        y = jnp.dot(h, w_ref[:, j * cn:(j + 1) * cn], preferred_element_type=F32)
        for t in range(cn // LANES):
            col = j * cn + t * LANES
            yt = y[:, t * LANES:(t + 1) * LANES]
            if col < n_prep:
                sq = yt * yt
                s_lo = jnp.sum(jnp.where(lo, sq, 0.0), axis=-1, keepdims=True)
                s_hi = jnp.sum(jnp.where(lo, 0.0, sq), axis=-1, keepdims=True)
                r = lax.rsqrt(jnp.where(lo, s_lo, s_hi) * (1.0 / 64.0) + EPS)
                yn = yt * r * geff_ref[:, col:col + LANES]
                partner = jnp.where(first, pltpu.roll(yn, LANES - 8, 1), pltpu.roll(yn, 8, 1))
                yt = yn * cos + partner * sin
            o_ref[0, :, col:col + LANES] = yt.astype(BF16)


def _project(x, gin, w, geff, cos, sin, n_prep):
    B, S, D = x.shape
    n_out = w.shape[1]
    tm = min(512, S)
    cn = 512
    kern = functools.partial(_proj_kernel, n_prep=n_prep, cn=cn)
    return pl.pallas_call(
        kern,
        grid=(B, S // tm),
        in_specs=[
            pl.BlockSpec((1, tm, D), lambda b, i: (b, i, 0)),
            _resident((1, D)),
            _resident((D, n_out)),
            _resident((1, n_prep)),
            pl.BlockSpec((tm, LANES), lambda b, i: (i, 0)),
            pl.BlockSpec((tm, LANES), lambda b, i: (i, 0)),
        ],
        out_specs=pl.BlockSpec((1, tm, n_out), lambda b, i: (b, i, 0)),
        out_shape=jax.ShapeDtypeStruct((B, S, n_out), BF16),
        compiler_params=_cparams(("parallel", "parallel")),
        name="proj",
    )(x, gin, w, geff, cos, sin)


def _outproj_kernel(a_ref, w_ref, x_ref, o_ref):
    o_ref[0] = x_ref[0] + jnp.dot(a_ref[0], w_ref[...], preferred_element_type=F32)


def _outproj(a, w, x):
    B, S, K = a.shape
    D = w.shape[1]
    tm = min(512, S)
    return pl.pallas_call(
        _outproj_kernel,
        grid=(B, S // tm),
        in_specs=[
            pl.BlockSpec((1, tm, K), lambda b, i: (b, i, 0)),
            _resident((K, D)),
            pl.BlockSpec((1, tm, D), lambda b, i: (b, i, 0)),
        ],
        out_specs=pl.BlockSpec((1, tm, D), lambda b, i: (b, i, 0)),
        out_shape=jax.ShapeDtypeStruct((B, S, D), F32),
        compiler_params=_cparams(("parallel", "parallel")),
        name="outproj",
    )(a, w, x)


def _da_kernel(q_ref, k_ref, v_ref, lq1_ref, lk1_ref, lq2_ref, lk2_ref, subg_ref, o_ref,
               qm_ref, m_ref, l_ref, acc_ref, *, lambda_init):
    ki = pl.program_id(3)
    tq = q_ref.shape[1]
    tk = k_ref.shape[1]

    @pl.when(ki == 0)
    def _init():
        q = q_ref[0]
        lane = lax.broadcasted_iota(jnp.int32, q.shape, 1)
        zero = jnp.zeros_like(q)
        qm_ref[0] = jnp.where(lane < DA_DK, q, zero)
        qm_ref[1] = jnp.where(lane < DA_DK, zero, q)
        m_ref[...] = jnp.full(m_ref.shape, -jnp.inf, F32)
        l_ref[...] = jnp.zeros(l_ref.shape, F32)
        acc_ref[...] = jnp.zeros(acc_ref.shape, F32)

    k = k_ref[0]
    v = v_ref[0]
    for mp in range(2):
        s = lax.dot_general(qm_ref[mp], k, (((1,), (1,)), ((), ())), preferred_element_type=F32)
        m_prev = m_ref[mp]
        m_new = jnp.maximum(m_prev, jnp.max(s, axis=-1, keepdims=True))
        alpha = jnp.exp(m_prev - m_new)
        p = jnp.exp(s - jnp.concatenate([m_new] * (tk // LANES), axis=1))
        l_ref[mp] = alpha * l_ref[mp] + jnp.sum(p, axis=-1, keepdims=True)
        acc_ref[mp] = alpha * acc_ref[mp] + jnp.dot(p.astype(BF16), v, preferred_element_type=F32)
        m_ref[mp] = m_new

    @pl.when(ki == pl.num_programs(3) - 1)
    def _fin():
        lam = (jnp.exp(jnp.sum(lq1_ref[...] * lk1_ref[...], axis=-1, keepdims=True))
               - jnp.exp(jnp.sum(lq2_ref[...] * lk2_ref[...], axis=-1, keepdims=True)) + lambda_init)
        o = acc_ref[0] / l_ref[0] - lam * (acc_ref[1] / l_ref[1])
        o = o * lax.rsqrt(jnp.mean(o * o, axis=-1, keepdims=True) + EPS)
        o_ref[0] = (o * (subg_ref[...] * (1.0 - lambda_init))).astype(BF16)


def _diff_attention(qkv, lq1, lk1, lq2, lk2, subg, lambda_init):
    B, S, _ = qkv.shape
    H = DA_HEADS
    tq = min(1024, S)
    tk = min(1024, S)
    kern = functools.partial(_da_kernel, lambda_init=lambda_init)
    vec = _resident((1, DA_DK))
    return pl.pallas_call(
        kern,
        grid=(B, H, S // tq, S // tk),
        in_specs=[
            pl.BlockSpec((1, tq, LANES), lambda b, h, i, j: (b, i, h)),
            pl.BlockSpec((1, tk, LANES), lambda b, h, i, j: (b, j, H + h)),
            pl.BlockSpec((1, tk, LANES), lambda b, h, i, j: (b, j, 2 * H + h)),
            vec, vec, vec, vec,
            _resident((1, LANES)),
        ],
        out_specs=pl.BlockSpec((1, tq, LANES), lambda b, h, i, j: (b, i, h)),
        out_shape=jax.ShapeDtypeStruct((B, S, H * LANES), BF16),
        scratch_shapes=[
            pltpu.VMEM((2, tq, LANES), BF16),
            pltpu.VMEM((2, tq, LANES), F32),
            pltpu.VMEM((2, tq, LANES), F32),
            pltpu.VMEM((2, tq, LANES), F32),
        ],
        compiler_params=_cparams(("parallel", "parallel", "parallel", "arbitrary")),
        name="diff_attn",
    )(qkv, qkv, qkv, lq1, lk1, lq2, lk2, subg)


def _wa_kernel(sink_ref, q_ref, kp_ref, kc_ref, kn_ref, vp_ref, vc_ref, vn_ref, o_ref, *, seq_len):
    i = pl.program_id(1)
    tq = q_ref.shape[1]
    nk = tq + 2 * WINDOW
    kcat = jnp.concatenate([kp_ref[0], kc_ref[0], kn_ref[0]], axis=0)
    vcat = jnp.concatenate([vp_ref[0], vc_ref[0], vn_ref[0]], axis=0)
    qpos = i * tq + lax.broadcasted_iota(jnp.int32, (tq, nk), 0)
    kpos = i * tq - WINDOW + lax.broadcasted_iota(jnp.int32, (tq, nk), 1)
    valid = (kpos >= 0) & (kpos < seq_len) & (jnp.abs(qpos - kpos) <= WINDOW)
    lane = lax.broadcasted_iota(jnp.int32, (tq, LANES), 1)
    lo = lane < WA_DH
    group = WA_HEADS // WA_KV_HEADS
    for pair in range(WA_HEADS // 2):
        qt = q_ref[0, :, pair * LANES:(pair + 1) * LANES]
        zero = jnp.zeros_like(qt)
        outs = []
        for half in range(2):
            h = 2 * pair + half
            g = h // group
            qh = jnp.where(lo, qt, zero) if half == 0 else jnp.where(lo, zero, qt)
            kg = kcat[:, g * LANES:(g + 1) * LANES]
            vg = vcat[:, g * LANES:(g + 1) * LANES]
            s = lax.dot_general(qh, kg, (((1,), (1,)), ((), ())), preferred_element_type=F32)
            s = jnp.where(valid, s, -jnp.inf)
            sink = sink_ref[h]
            m = jnp.maximum(jnp.max(s, axis=-1, keepdims=True), sink)
            p = jnp.exp(s - m)
            denom = jnp.sum(p, axis=-1, keepdims=True) + jnp.exp(sink - m)
            outs.append(jnp.dot(p.astype(BF16), vg, preferred_element_type=F32) / denom)
        o_ref[0, :, pair * LANES:(pair + 1) * LANES] = jnp.where(lo, outs[0], outs[1]).astype(BF16)


def _window_attention(qkv, sink):
    B, S, _ = qkv.shape
    tq = min(256, S)
    qw = WA_HEADS * WA_DH
    kw = WA_KV_HEADS * LANES
    qb, kb, vb = 0, qw // kw, qw // kw + 1
    r = tq // WINDOW
    last = S // WINDOW - 1
    prev = lambda b, i: (b, jnp.maximum(i * r - 1, 0))
    nxt = lambda b, i: (b, jnp.minimum((i + 1) * r, last))
    kern = functools.partial(_wa_kernel, seq_len=S)
    return pl.pallas_call(
        kern,
        grid=(B, S // tq),
        in_specs=[
            pl.BlockSpec(memory_space=pltpu.SMEM),
            pl.BlockSpec((1, tq, qw), lambda b, i: (b, i, qb)),
            pl.BlockSpec((1, WINDOW, kw), lambda b, i: prev(b, i) + (kb,)),
            pl.BlockSpec((1, tq, kw), lambda b, i: (b, i, kb)),
            pl.BlockSpec((1, WINDOW, kw), lambda b, i: nxt(b, i) + (kb,)),
            pl.BlockSpec((1, WINDOW, kw), lambda b, i: prev(b, i) + (vb,)),
            pl.BlockSpec((1, tq, kw), lambda b, i: (b, i, vb)),
            pl.BlockSpec((1, WINDOW, kw), lambda b, i: nxt(b, i) + (vb,)),
        ],
        out_specs=pl.BlockSpec((1, tq, qw), lambda b, i: (b, i, 0)),
        out_shape=jax.ShapeDtypeStruct((B, S, qw), BF16),
        compiler_params=_cparams(("parallel", "parallel")),
        name="window_attn",
    )(sink, qkv, qkv, qkv, qkv, qkv, qkv, qkv)


def _fill_hext(hext, xp_ref, x_ref, xn_ref, gin):
    i = pl.program_id(1)
    tm = x_ref.shape[1]
    hp = _rmsnorm_bf16(xp_ref[0], gin)
    hn = _rmsnorm_bf16(xn_ref[0], gin)
    hext[0:HALO] = jnp.where(i > 0, hp, jnp.zeros_like(hp))
    hext[HALO:HALO + tm] = _rmsnorm_bf16(x_ref[0], gin)
    hext[HALO + tm:] = jnp.where(i < pl.num_programs(1) - 1, hn, jnp.zeros_like(hn))


def _halo_specs(S, tm, D):
    r = tm // HALO
    last = S // HALO - 1
    return [
        pl.BlockSpec((1, HALO, D), lambda b, i: (b, jnp.maximum(i * r - 1, 0), 0)),
        pl.BlockSpec((1, tm, D), lambda b, i: (b, i, 0)),
        pl.BlockSpec((1, HALO, D), lambda b, i: (b, jnp.minimum((i + 1) * r, last), 0)),
    ]


def _shift_rows(y, off, tm):
    if off == 0:
        return y[HALO:HALO + tm]
    return pltpu.roll(y, (-off) % y.shape[0], 0)[HALO:HALO + tm]


def _ffn_kernel(xp_ref, x_ref, xn_ref, gin_ref, wg_ref, wu_ref, cw_ref, wd_ref, o_ref, hext, acc):
    tm = x_ref.shape[1]
    _fill_hext(hext, xp_ref, x_ref, xn_ref, gin_ref[...])
    acc[...] = x_ref[0]

    def body(c, carry):
        g = jnp.dot(hext[...], wg_ref[c], preferred_element_type=F32)
        u = jnp.dot(hext[HALO:HALO + tm], wu_ref[c], preferred_element_type=F32)
        cw = cw_ref[c]
        gc = (_shift_rows(g, -1, tm) * cw[0:1] + _shift_rows(g, 0, tm) * cw[1:2]
              + _shift_rows(g, 1, tm) * cw[2:3] + cw[3:4])
        act = (gc * jax.nn.sigmoid(gc) * u).astype(BF16)
        acc[...] += jnp.dot(act, wd_ref[c], preferred_element_type=F32)
        return carry

    lax.fori_loop(0, wg_ref.shape[0], body, 0)
    o_ref[0] = acc[...]


def _conv_ffn(x, gin, wg, wu, cw, wd):
    B, S, D = x.shape
    nc, _, fc = wg.shape
    tm = min(512, S)
    return pl.pallas_call(
        _ffn_kernel,
        grid=(B, S // tm),
        in_specs=_halo_specs(S, tm, D) + [
            _resident((1, D)),
            _resident((nc, D, fc)),
            _resident((nc, D, fc)),
            _resident((nc, 8, fc)),
            _resident((nc, fc, D)),
        ],
        out_specs=pl.BlockSpec((1, tm, D), lambda b, i: (b, i, 0)),
        out_shape=jax.ShapeDtypeStruct((B, S, D), F32),
        scratch_shapes=[pltpu.VMEM((tm + 2 * HALO, D), BF16), pltpu.VMEM((tm, D), F32)],
        compiler_params=_cparams(("parallel", "parallel")),
        name="conv_ffn",
    )(x, x, x, gin, wg, wu, cw, wd)


def _rgin_kernel(xp_ref, x_ref, xn_ref, gin_ref, wg_ref, wu_ref, cw_ref, gate_ref, u_ref, hext, *, cn):
    tm = x_ref.shape[1]
    _fill_hext(hext, xp_ref, x_ref, xn_ref, gin_ref[...])
    n_out = wg_ref.shape[1]
    for j in range(n_out // cn):
        cols = slice(j * cn, (j + 1) * cn)
        gate_ref[0, :, cols] = jnp.dot(hext[HALO:HALO + tm], wg_ref[:, cols], preferred_element_type=F32)
        u = jnp.dot(hext[...], wu_ref[:, cols], preferred_element_type=F32)
        cw = cw_ref[:, cols]
        u_ref[0, :, cols] = (_shift_rows(u, -2, tm) * cw[0:1] + _shift_rows(u, -1, tm) * cw[1:2]
                             + _shift_rows(u, 0, tm) * cw[2:3] + _shift_rows(u, 1, tm) * cw[3:4] + cw[4:5])


def _rg_in(x, gin, wg, wu, cw):
    B, S, D = x.shape
    C = wg.shape[1]
    tm = min(512, S)
    kern = functools.partial(_rgin_kernel, cn=512)
    out = jax.ShapeDtypeStruct((B, S, C), F32)
    ospec = pl.BlockSpec((1, tm, C), lambda b, i: (b, i, 0))
    return pl.pallas_call(
        kern,
        grid=(B, S // tm),
        in_specs=_halo_specs(S, tm, D) + [
            _resident((1, D)), _resident((D, C)), _resident((D, C)), _resident((8, C))],
        out_specs=[ospec, ospec],
        out_shape=[out, out],
        scratch_shapes=[pltpu.VMEM((tm + 2 * HALO, D), BF16)],
        compiler_params=_cparams(("parallel", "parallel")),
        name="rg_in",
    )(x, x, x, gin, wg, wu, cw)


def _rg_gate_tiles(C, bw):
    tiles = []
    for j in range(C // MXU_DIM):
        b_lo = (j * MXU_DIM) // bw
        b_hi = (j * MXU_DIM + MXU_DIM - 1) // bw
        k_lo = (b_lo * bw) // LANES * LANES
        k_hi = min(C, -(-((b_hi + 1) * bw) // LANES) * LANES)
        tiles.append((k_lo, k_hi))
    return tiles


def _rgscan_kernel(*refs, reverse, final, bw):
    if final:
        (u_ref, wr_ref, wi_ref, gb_ref, hf_ref, gate_ref, x_ref, wo_ref, o_ref,
         a_s, b_s, h_s, carry) = refs
    else:
        u_ref, wr_ref, wi_ref, gb_ref, o_ref, a_s, b_s, h_s, carry = refs
    tc, C = u_ref.shape[1], u_ref.shape[2]

    @pl.when(pl.program_id(1) == 0)
    def _init():
        carry[...] = jnp.zeros(carry.shape, F32)

    u = u_ref[0]
    ub = u.astype(BF16)
    for j, (k_lo, k_hi) in enumerate(_rg_gate_tiles(C, bw)):
        cols = slice(j * MXU_DIM, (j + 1) * MXU_DIM)
        rp = jnp.dot(ub[:, k_lo:k_hi], wr_ref[k_lo:k_hi, cols], preferred_element_type=F32) + gb_ref[0:1, cols]
        ip = jnp.dot(ub[:, k_lo:k_hi], wi_ref[k_lo:k_hi, cols], preferred_element_type=F32) + gb_ref[1:2, cols]
        log_a = (-RG_C * jax.nn.softplus(-gb_ref[2:3, cols])) * jax.nn.sigmoid(rp)
        a = jnp.exp(log_a)
        a_s[:, cols] = a
        b_s[:, cols] = jnp.sqrt(-jnp.tanh(log_a) * (a * a + 1.0)) * (jax.nn.sigmoid(ip) * u[:, cols])

    ng = tc // 8

    def body(gi, h):
        g = ng - 1 - gi if reverse else gi
        r0 = pl.multiple_of(g * 8, 8)
        a8 = a_s[pl.ds(r0, 8), :]
        b8 = b_s[pl.ds(r0, 8), :]
        rows = [None] * 8
        for t in (range(7, -1, -1) if reverse else range(8)):
            h = a8[t:t + 1] * h + b8[t:t + 1]
            rows[t] = h
        h_s[pl.ds(r0, 8), :] = jnp.concatenate(rows, axis=0)
        return h

    carry[0:1, :] = lax.fori_loop(0, ng, body, carry[0:1, :])
    if final:
        y = ((hf_ref[0] + h_s[...]) * jax.nn.gelu(gate_ref[0])).astype(BF16)
        o_ref[0] = x_ref[0] + jnp.dot(y, wo_ref[...], preferred_element_type=F32)
    else:
        o_ref[0] = h_s[...]


def _rg_scan(u, wr, wi, gb, *, reverse, extra=None):
    B, S, C = u.shape
    tc = min(256, S)
    n = S // tc
    idx = (lambda b, i: (b, n - 1 - i, 0)) if reverse else (lambda b, i: (b, i, 0))
    final = extra is not None
    in_specs = [pl.BlockSpec((1, tc, C), idx), _resident((C, C)), _resident((C, C)), _resident((8, C))]
    args = [u, wr, wi, gb]
    out_w = C
    if final:
        hf, gate, x, wo = extra
        out_w = x.shape[2]
        in_specs += [pl.BlockSpec((1, tc, C), idx), pl.BlockSpec((1, tc, C), idx),
                     pl.BlockSpec((1, tc, out_w), idx), _resident(wo.shape)]
        args += [hf, gate, x, wo]
    kern = functools.partial(_rgscan_kernel, reverse=reverse, final=final, bw=C // RG_BLOCKS)
    return pl.pallas_call(
        kern,
        grid=(B, n),
        in_specs=in_specs,
        out_specs=pl.BlockSpec((1, tc, out_w), idx),
        out_shape=jax.ShapeDtypeStruct((B, S, out_w), F32),
        scratch_shapes=[pltpu.VMEM((tc, C), F32), pltpu.VMEM((tc, C), F32), pltpu.VMEM((tc, C), F32),
                        pltpu.VMEM((8, C), F32)],
        compiler_params=_cparams(("parallel", "arbitrary")),
        name="rg_scan_final" if final else "rg_scan",
    )(*args)


def _rope_tables(S, dh):
    rot = dh // ROT_FRAC
    half = rot // 2
    inv = ROPE_THETA ** (-jnp.arange(half, dtype=F32) * 2.0 / rot)
    ang = jnp.arange(S, dtype=F32)[:, None] * inv[None, :]
    cos, sin = jnp.cos(ang), jnp.sin(ang)
    rest = dh - rot
    c = jnp.concatenate([cos, cos, jnp.ones((S, rest), F32)], axis=1)
    s = jnp.concatenate([-sin, sin, jnp.zeros((S, rest), F32)], axis=1)
    reps = LANES // dh
    return jnp.tile(c, (1, reps)), jnp.tile(s, (1, reps))


def _pad_rows(a, rows=8):
    return jnp.pad(a, ((0, rows - a.shape[0]), (0, 0)))


def _block_diag(w):
    n, c, e = w.shape
    eye = jnp.eye(n, dtype=w.dtype)
    return (w[:, :, None, :] * eye[:, None, :, None]).reshape(n * c, n * e)


def _prepare(p):
    depth, D = p['norm_mix'].shape
    F = p['ffn_w_down'].shape[1]
    fc = MXU_DIM
    nc = F // fc
    prep = {'norm_mix': p['norm_mix'].reshape(depth, 1, D), 'norm_ffn': p['norm_ffn'].reshape(depth, 1, D)}
    wup = p['ffn_w_up'].astype(BF16)
    prep['ffn_wg'] = wup[:, :, :F].reshape(depth, D, nc, fc).transpose(0, 2, 1, 3)
    prep['ffn_wu'] = wup[:, :, F:].reshape(depth, D, nc, fc).transpose(0, 2, 1, 3)
    cw = jnp.concatenate([p['ffn_conv_w'], p['ffn_conv_b'][:, None, :]], axis=1)
    cw = jnp.pad(cw, ((0, 0), (0, 8 - cw.shape[1]), (0, 0)))
    prep['ffn_cw'] = cw.reshape(depth, 8, nc, fc).transpose(0, 2, 1, 3)
    prep['ffn_wd'] = p['ffn_w_down'].astype(BF16).reshape(depth, nc, fc, D)

    n_da_q = DA_HEADS * 2
    prep['da_w_qkv'] = p['da_w_qkv'].astype(BF16)
    prep['da_geff'] = jnp.concatenate([jnp.tile(p['da_q_norm'] * (DA_DK ** -0.5), (1, n_da_q)),
                                       jnp.tile(p['da_k_norm'], (1, n_da_q))], axis=1)[:, None, :]
    prep['da_w_o'] = p['da_w_o'].astype(BF16)
    for name in ('da_lambda_q1', 'da_lambda_k1', 'da_lambda_q2', 'da_lambda_k2', 'da_sub_norm'):
        prep[name] = p[name][:, None, :]

    qw, kvw = WA_HEADS * WA_DH, WA_KV_HEADS * WA_DH
    w = p['wa_w_qkv']
    n_b = w.shape[0]

    def dup(cols):
        c = cols.reshape(n_b, D, WA_KV_HEADS, 1, WA_DH)
        return jnp.broadcast_to(c, (n_b, D, WA_KV_HEADS, 2, WA_DH)).reshape(n_b, D, 2 * kvw)

    prep['wa_w_qkv'] = jnp.concatenate(
        [w[:, :, :qw], dup(w[:, :, qw:qw + kvw]), dup(w[:, :, qw + kvw:])], axis=2).astype(BF16)
    prep['wa_geff'] = jnp.concatenate([jnp.tile(p['wa_q_norm'] * (WA_DH ** -0.5), (1, WA_HEADS)),
                                       jnp.tile(p['wa_k_norm'], (1, 2 * WA_KV_HEADS))], axis=1)[:, None, :]
    prep['wa_sink'] = p['wa_sink']
    prep['wa_w_o'] = p['wa_w_o'].astype(BF16)

    C = p['rg_w_out'].shape[1]
    bw = C // RG_BLOCKS
    win = p['rg_w_in'].astype(BF16)
    prep['rg_wg'], prep['rg_wu'] = win[:, :, :C], win[:, :, C:]
    rcw = jnp.concatenate([p['rg_conv_w'], p['rg_conv_b'][:, None, :]], axis=1)
    prep['rg_cw'] = jnp.pad(rcw, ((0, 0), (0, 8 - rcw.shape[1]), (0, 0)))
    gw, gb = p['rg_gate_w'], p['rg_gate_b']
    n_c = gw.shape[0]
    prep['rg_wr'] = jnp.stack([jnp.stack([_block_diag(gw[j, d, :, :, :bw]) for d in range(2)])
                               for j in range(n_c)]).astype(BF16)
    prep['rg_wi'] = jnp.stack([jnp.stack([_block_diag(gw[j, d, :, :, bw:]) for d in range(2)])
                               for j in range(n_c)]).astype(BF16)
    gvec = jnp.stack([gb[..., :bw].reshape(n_c, 2, C), gb[..., bw:].reshape(n_c, 2, C), p['rg_lambda']], axis=2)
    prep['rg_gb'] = jnp.pad(gvec, ((0, 0), (0, 0), (0, 5), (0, 0)))
    prep['rg_w_out'] = p['rg_w_out'].astype(BF16)
    return prep


def _lambda_init(layer_idx):
    return 0.8 - 0.6 * math.exp(-0.3 * layer_idx)


def _trunk(x, w):
    S = x.shape[1]
    cos, sin = _rope_tables(S, DA_DK)
    depth = w['norm_mix'].shape[0]
    for i in range(depth):
        kind, j = i % N_MIXERS, i // N_MIXERS
        gin = w['norm_mix'][i]
        if kind == 0:
            qkv = _project(x, gin, w['da_w_qkv'][j], w['da_geff'][j], cos, sin, 2 * DA_HEADS * LANES)
            o = _diff_attention(qkv, w['da_lambda_q1'][j], w['da_lambda_k1'][j], w['da_lambda_q2'][j],
                                w['da_lambda_k2'][j], w['da_sub_norm'][j], _lambda_init(i))
            x = _outproj(o, w['da_w_o'][j], x)
        elif kind == 1:
            n_prep = WA_HEADS * WA_DH + WA_KV_HEADS * LANES
            qkv = _project(x, gin, w['wa_w_qkv'][j], w['wa_geff'][j], cos, sin, n_prep)
            o = _window_attention(qkv, w['wa_sink'][j])
            x = _outproj(o, w['wa_w_o'][j], x)
        else:
            gate, u = _rg_in(x, gin, w['rg_wg'][j], w['rg_wu'][j], w['rg_cw'][j])
            hf = _rg_scan(u, w['rg_wr'][j, 0], w['rg_wi'][j, 0], w['rg_gb'][j, 0], reverse=False)
            x = _rg_scan(u, w['rg_wr'][j, 1], w['rg_wi'][j, 1], w['rg_gb'][j, 1], reverse=True,
                         extra=(hf, gate, x, w['rg_w_out'][j]))
        x = _conv_ffn(x, w['norm_ffn'][i], w['ffn_wg'][i], w['ffn_wu'][i], w['ffn_cw'][i], w['ffn_wd'][i])
    return x


def kernel(x_prompt, x_sample, norm_mix, norm_ffn, ffn_w_up, ffn_conv_w, ffn_conv_b, ffn_w_down, da_w_qkv, da_q_norm, da_k_norm, da_lambda_q1, da_lambda_k1, da_lambda_q2, da_lambda_k2, da_sub_norm, da_w_o, wa_w_qkv, wa_q_norm, wa_k_norm, wa_sink, wa_w_o, rg_w_in, rg_conv_w, rg_conv_b, rg_gate_w, rg_gate_b, rg_lambda, rg_w_out):
    w = _prepare(dict(
        norm_mix=norm_mix, norm_ffn=norm_ffn, ffn_w_up=ffn_w_up, ffn_conv_w=ffn_conv_w, ffn_conv_b=ffn_conv_b,
        ffn_w_down=ffn_w_down, da_w_qkv=da_w_qkv, da_q_norm=da_q_norm, da_k_norm=da_k_norm,
        da_lambda_q1=da_lambda_q1, da_lambda_k1=da_lambda_k1, da_lambda_q2=da_lambda_q2,
        da_lambda_k2=da_lambda_k2, da_sub_norm=da_sub_norm, da_w_o=da_w_o, wa_w_qkv=wa_w_qkv,
        wa_q_norm=wa_q_norm, wa_k_norm=wa_k_norm, wa_sink=wa_sink, wa_w_o=wa_w_o, rg_w_in=rg_w_in,
        rg_conv_w=rg_conv_w, rg_conv_b=rg_conv_b, rg_gate_w=rg_gate_w, rg_gate_b=rg_gate_b,
        rg_lambda=rg_lambda, rg_w_out=rg_w_out))
    return (_trunk(x_prompt, w), _trunk(x_sample, w))
```

```python
import functools
import math

import jax
import jax.numpy as jnp
from jax import lax
from jax.experimental import pallas as pl
from jax.experimental.pallas import tpu as pltpu

F32 = jnp.float32
BF16 = jnp.bfloat16

DA_HEADS = 8
DA_DK = 64
WA_HEADS = 16
WA_KV_HEADS = 4
WA_DH = 64
WINDOW = 128
RG_BLOCKS = 16
RG_C = 8.0
ROPE_THETA = 500000.0
ROT_FRAC = 4
EPS = 1e-6
N_MIXERS = 3
LOG2E = math.log2(math.e)

LANES = 128
MXU_DIM = 256
HALO = 16
ROW_CHUNK = 64
ONES_ROWS = 16
VMEM_LIMIT = 56 * 1024 * 1024


def _cparams(sem):
    return pltpu.CompilerParams(dimension_semantics=sem, vmem_limit_bytes=VMEM_LIMIT)


def _rmsnorm_bf16(x, g):
    return (x * lax.rsqrt(jnp.mean(x * x, axis=-1, keepdims=True) + EPS) * g).astype(BF16)


def _resident(shape):
    nd = len(shape)
    return pl.BlockSpec(shape, lambda *_: (0,) * nd)


def _proj_kernel(*refs, n_prep, cn, with_vt):
    if with_vt:
        x_ref, gin_ref, w_ref, geff_ref, cos_ref, sin_ref, wvt_ref, o_ref, vt_ref = refs
    else:
        x_ref, gin_ref, w_ref, geff_ref, cos_ref, sin_ref, o_ref = refs
    tm = x_ref.shape[1]
    n_out = w_ref.shape[1]
    h = _rmsnorm_bf16(x_ref[0], gin_ref[...])
    lane = lax.broadcasted_iota(jnp.int32, (tm, LANES), 1)
    lo = lane < 64
    first = (lane & 63) < 8
    cos = cos_ref[...]
    sin = sin_ref[...]
    for j in range(n_out // cn):
        y = jnp.dot(h, w_ref[:, j * cn:(j + 1) * cn], preferred_element_type=F32)
        for t in range(cn // LANES):
            col = j * cn + t * LANES
            yt = y[:, t * LANES:(t + 1) * LANES]
            if col < n_prep:
                sq = yt * yt
                s_lo = jnp.sum(jnp.where(lo, sq, 0.0), axis=-1, keepdims=True)
                s_hi = jnp.sum(jnp.where(lo, 0.0, sq), axis=-1, keepdims=True)
                r = lax.rsqrt(jnp.where(lo, s_lo, s_hi) * (1.0 / 64.0) + EPS)
                yn = yt * r * geff_ref[:, col:col + LANES]
                partner = jnp.where(first, pltpu.roll(yn, LANES - 8, 1), pltpu.roll(yn, 8, 1))
                yt = yn * cos + partner * sin
            o_ref[0, :, col:col + LANES] = yt.astype(BF16)
    if with_vt:
        n_vt = wvt_ref.shape[0]
        dv = LANES
        for j in range(n_vt // cn):
            vt = lax.dot_general(wvt_ref[j * cn:(j + 1) * cn, :], h, (((1,), (1,)), ((), ())),
                                 preferred_element_type=F32).astype(BF16)
            for t in range(cn // dv):
                r0 = (j * (cn // dv) + t) * (dv + ONES_ROWS)
                vt_ref[0, 0, r0:r0 + dv, :] = vt[t * dv:(t + 1) * dv]
                vt_ref[0, 0, r0 + dv:r0 + dv + ONES_ROWS, :] = jnp.ones((ONES_ROWS, tm), BF16)


def _project(x, gin, w, geff, cos, sin, n_prep, wvt=None):
    B, S, D = x.shape
    n_out = w.shape[1]
    tm = min(1024 if wvt is not None else 512, S)
    cn = 512
    with_vt = wvt is not None
    kern = functools.partial(_proj_kernel, n_prep=n_prep, cn=cn, with_vt=with_vt)
    in_specs = [
        pl.BlockSpec((1, tm, D), lambda b, i: (b, i, 0)),
        _resident((1, D)),
        _resident((D, n_out)),
        _resident((1, n_prep)),
        pl.BlockSpec((tm, LANES), lambda b, i: (i, 0)),
        pl.BlockSpec((tm, LANES), lambda b, i: (i, 0)),
    ]
    args = [x, gin, w, geff, cos, sin]
    out_specs = pl.BlockSpec((1, tm, n_out), lambda b, i: (b, i, 0))
    out_shape = jax.ShapeDtypeStruct((B, S, n_out), BF16)
    if with_vt:
        n_vt = wvt.shape[0]
        vt_rows = n_vt // LANES * (LANES + ONES_ROWS)
        in_specs.append(_resident((n_vt, D)))
        args.append(wvt)
        out_specs = [out_specs, pl.BlockSpec((1, 1, vt_rows, tm), lambda b, i: (b, i, 0, 0))]
        out_shape = [out_shape, jax.ShapeDtypeStruct((B, S // tm, vt_rows, tm), BF16)]
    return pl.pallas_call(
        kern,
        grid=(B, S // tm),
        in_specs=in_specs,
        out_specs=out_specs,
        out_shape=out_shape,
        compiler_params=_cparams(("parallel", "parallel")),
        name="proj_vt" if with_vt else "proj",
    )(*args)


def _outproj_kernel(a_ref, w_ref, x_ref, o_ref):
    o_ref[0] = x_ref[0] + jnp.dot(a_ref[0], w_ref[...], preferred_element_type=F32)


def _outproj(a, w, x):
    B, S, K = a.shape
    D = w.shape[1]
    tm = min(512, S)
    return pl.pallas_call(
        _outproj_kernel,
        grid=(B, S // tm),
        in_specs=[
            pl.BlockSpec((1, tm, K), lambda b, i: (b, i, 0)),
            _resident((K, D)),
            pl.BlockSpec((1, tm, D), lambda b, i: (b, i, 0)),
        ],
        out_specs=pl.BlockSpec((1, tm, D), lambda b, i: (b, i, 0)),
        out_shape=jax.ShapeDtypeStruct((B, S, D), F32),
        compiler_params=_cparams(("parallel", "parallel")),
        name="outproj",
    )(a, w, x)


def _da_kernel(q_ref, k_ref, vt_ref, lq1_ref, lk1_ref, lq2_ref, lk2_ref, subg_ref, o_ref,
               qm_ref, s_buf, mx_buf, p_buf, m_ref, acc_ref, *, lambda_init):
    nj = qm_ref.shape[1]
    nkt, _, tk = vt_ref.shape[1:]
    n_units = nkt * nj
    dv = LANES

    q = q_ref[0]
    lane = lax.broadcasted_iota(jnp.int32, q.shape, 1)
    zero = jnp.zeros_like(q)
    for mp, qm in enumerate((jnp.where(lane < DA_DK, q, zero), jnp.where(lane < DA_DK, zero, q))):
        for j in range(nj):
            qm_ref[mp, j] = qm[j * MXU_DIM:(j + 1) * MXU_DIM]
    m_ref[...] = jnp.full(m_ref.shape, -jnp.inf, F32)
    acc_ref[...] = jnp.zeros(acc_ref.shape, F32)

    def scores(t, slot):
        kt, j = t // nj, t % nj
        kb = k_ref[0, pl.ds(pl.multiple_of(kt * tk, tk), tk), :]
        for mp in range(2):
            s = lax.dot_general(kb, qm_ref[mp, j], (((1,), (1,)), ((), ())),
                                preferred_element_type=F32)
            s_buf[slot, mp] = s
            mx_buf[slot, mp] = col_reduce(s, jnp.maximum)

    def col_reduce(x, op):
        acc = x[0:ROW_CHUNK]
        for r in range(1, x.shape[0] // ROW_CHUNK):
            acc = op(acc, x[r * ROW_CHUNK:(r + 1) * ROW_CHUNK])
        rows = ROW_CHUNK
        while rows > 8:
            rows //= 2
            acc = op(acc[:rows], acc[rows:])
        red = (jnp.max if op is jnp.maximum else jnp.sum)(acc, axis=0, keepdims=True)
        return jnp.broadcast_to(red, (8, x.shape[1]))

    def softmax_pv(t, slot):
        kt, j = t // nj, t % nj
        vt = vt_ref[0, kt]
        for mp in range(2):
            m_prev = m_ref[mp, j]
            m_new = jnp.maximum(m_prev, mx_buf[slot, mp])
            alpha = jnp.exp2(m_prev - m_new)
            p_buf[slot, mp] = jnp.exp2(s_buf[slot, mp] - m_new[0:1]).astype(BF16)
            acc_ref[mp, j] = alpha[0:1] * acc_ref[mp, j] + jnp.dot(vt, p_buf[slot, mp],
                                                                   preferred_element_type=F32)
            m_ref[mp, j] = m_new

    unroll = 2
    scores(0, 0)
    scores(1, 1)

    def body(i, carry):
        for u in range(unroll):
            softmax_pv(unroll * i + u, u % 2)
            scores(unroll * i + u + 2, u % 2)
        return carry

    lax.fori_loop(0, n_units // unroll - 1, body, 0)
    for u in range(unroll):
        softmax_pv(n_units - unroll + u, u % 2)
        if u + 2 < unroll:
            scores(n_units - unroll + u + 2, u % 2)

    lam = (jnp.exp(jnp.sum(lq1_ref[...] * lk1_ref[...], axis=-1, keepdims=True))
           - jnp.exp(jnp.sum(lq2_ref[...] * lk2_ref[...], axis=-1, keepdims=True)) + lambda_init)
    gain = subg_ref[...] * (1.0 - lambda_init)
    for j in range(nj):
        a0, a1 = acc_ref[0, j], acc_ref[1, j]
        ot = a0[:dv] / a0[dv:dv + 1] - lam * (a1[:dv] / a1[dv:dv + 1])
        o = ot.T
        o = o * lax.rsqrt(jnp.mean(o * o, axis=-1, keepdims=True) + EPS)
        o_ref[0, j * MXU_DIM:(j + 1) * MXU_DIM, :] = (o * gain).astype(BF16)


def _diff_attention(qk, vt, lq1, lk1, lq2, lk2, subg, lambda_init):
    B, S, _ = qk.shape
    _, nkt, hv, tk = vt.shape
    H = DA_HEADS
    vrows = hv // H
    tq = min(1024, S)
    nj = tq // MXU_DIM
    kern = functools.partial(_da_kernel, lambda_init=lambda_init)
    vec = _resident((1, DA_DK))
    return pl.pallas_call(
        kern,
        grid=(B, H, S // tq),
        in_specs=[
            pl.BlockSpec((1, tq, LANES), lambda b, h, i: (b, i, h)),
            pl.BlockSpec((1, S, LANES), lambda b, h, i: (b, 0, H + h)),
            pl.BlockSpec((1, nkt, vrows, tk), lambda b, h, i: (b, 0, h, 0)),
            vec, vec, vec, vec,
            _resident((1, LANES)),
        ],
        out_specs=pl.BlockSpec((1, tq, LANES), lambda b, h, i: (b, i, h)),
        out_shape=jax.ShapeDtypeStruct((B, S, H * LANES), BF16),
        scratch_shapes=[
            pltpu.VMEM((2, nj, MXU_DIM, LANES), BF16),
            pltpu.VMEM((2, 2, tk, MXU_DIM), F32),
            pltpu.VMEM((2, 2, 8, MXU_DIM), F32),
            pltpu.VMEM((2, 2, tk, MXU_DIM), BF16),
            pltpu.VMEM((2, nj, 8, MXU_DIM), F32),
            pltpu.VMEM((2, nj, vrows, MXU_DIM), F32),
        ],
        compiler_params=_cparams(("parallel", "parallel", "arbitrary")),
        name="diff_attn",
    )(qk, qk, vt, lq1, lk1, lq2, lk2, subg)


def _wa_kernel(sink_ref, q_ref, kp_ref, kc_ref, kn_ref, vp_ref, vc_ref, vn_ref, o_ref, *, seq_len):
    i = pl.program_id(1)
    tq = q_ref.shape[1]
    nk = tq + 2 * WINDOW
    kcat = jnp.concatenate([kp_ref[0], kc_ref[0], kn_ref[0]], axis=0)
    vcat = jnp.concatenate([vp_ref[0], vc_ref[0], vn_ref[0]], axis=0)
    qpos = i * tq + lax.broadcasted_iota(jnp.int32, (tq, nk), 0)
    kpos = i * tq - WINDOW + lax.broadcasted_iota(jnp.int32, (tq, nk), 1)
    valid = (kpos >= 0) & (kpos < seq_len) & (jnp.abs(qpos - kpos) <= WINDOW)
    lane = lax.broadcasted_iota(jnp.int32, (tq, LANES), 1)
    lo = lane < WA_DH
    group = WA_HEADS // WA_KV_HEADS
    for pair in range(WA_HEADS // 2):
        qt = q_ref[0, :, pair * LANES:(pair + 1) * LANES]
        zero = jnp.zeros_like(qt)
        outs = []
        for half in range(2):
            h = 2 * pair + half
            g = h // group
            qh = jnp.where(lo, qt, zero) if half == 0 else jnp.where(lo, zero, qt)
            kg = kcat[:, g * LANES:(g + 1) * LANES]
            vg = vcat[:, g * LANES:(g + 1) * LANES]
            s = lax.dot_general(qh, kg, (((1,), (1,)), ((), ())), preferred_element_type=F32)
            s = jnp.where(valid, s, -jnp.inf)
            sink = sink_ref[h]
            m = jnp.maximum(jnp.max(s, axis=-1, keepdims=True), sink)
            p = jnp.exp(s - m)
            denom = jnp.sum(p, axis=-1, keepdims=True) + jnp.exp(sink - m)
            outs.append(jnp.dot(p.astype(BF16), vg, preferred_element_type=F32) / denom)
        o_ref[0, :, pair * LANES:(pair + 1) * LANES] = jnp.where(lo, outs[0], outs[1]).astype(BF16)


def _window_attention(qkv, sink):
    B, S, _ = qkv.shape
    tq = min(256, S)
    qw = WA_HEADS * WA_DH
    kw = WA_KV_HEADS * LANES
    qb, kb, vb = 0, qw // kw, qw // kw + 1
    r = tq // WINDOW
    last = S // WINDOW - 1
    prev = lambda b, i: (b, jnp.maximum(i * r - 1, 0))
    nxt = lambda b, i: (b, jnp.minimum((i + 1) * r, last))
    kern = functools.partial(_wa_kernel, seq_len=S)
    return pl.pallas_call(
        kern,
        grid=(B, S // tq),
        in_specs=[
            pl.BlockSpec(memory_space=pltpu.SMEM),
            pl.BlockSpec((1, tq, qw), lambda b, i: (b, i, qb)),
            pl.BlockSpec((1, WINDOW, kw), lambda b, i: prev(b, i) + (kb,)),
            pl.BlockSpec((1, tq, kw), lambda b, i: (b, i, kb)),
            pl.BlockSpec((1, WINDOW, kw), lambda b, i: nxt(b, i) + (kb,)),
            pl.BlockSpec((1, WINDOW, kw), lambda b, i: prev(b, i) + (vb,)),
            pl.BlockSpec((1, tq, kw), lambda b, i: (b, i, vb)),
            pl.BlockSpec((1, WINDOW, kw), lambda b, i: nxt(b, i) + (vb,)),
        ],
        out_specs=pl.BlockSpec((1, tq, qw), lambda b, i: (b, i, 0)),
        out_shape=jax.ShapeDtypeStruct((B, S, qw), BF16),
        compiler_params=_cparams(("parallel", "parallel")),
        name="window_attn",
    )(sink, qkv, qkv, qkv, qkv, qkv, qkv, qkv)


def _fill_hext(hext, xp_ref, x_ref, xn_ref, gin):
    i = pl.program_id(1)
    tm = x_ref.shape[1]
    hp = _rmsnorm_bf16(xp_ref[0], gin)
    hn = _rmsnorm_bf16(xn_ref[0], gin)
    hext[0:HALO] = jnp.where(i > 0, hp, jnp.zeros_like(hp))
    hext[HALO:HALO + tm] = _rmsnorm_bf16(x_ref[0], gin)
    hext[HALO + tm:] = jnp.where(i < pl.num_programs(1) - 1, hn, jnp.zeros_like(hn))


def _halo_specs(S, tm, D):
    r = tm // HALO
    last = S // HALO - 1
    return [
        pl.BlockSpec((1, HALO, D), lambda b, i: (b, jnp.maximum(i * r - 1, 0), 0)),
        pl.BlockSpec((1, tm, D), lambda b, i: (b, i, 0)),
        pl.BlockSpec((1, HALO, D), lambda b, i: (b, jnp.minimum((i + 1) * r, last), 0)),
    ]


def _shift_rows(y, off, tm):
    if off == 0:
        return y[HALO:HALO + tm]
    return pltpu.roll(y, (-off) % y.shape[0], 0)[HALO:HALO + tm]


def _ffn_kernel(xp_ref, x_ref, xn_ref, gin_ref, wg_ref, wu_ref, cw_ref, wd_ref, o_ref, hext, g_buf, u_buf, acc):
    tm = x_ref.shape[1]
    nc = wg_ref.shape[0]
    _fill_hext(hext, xp_ref, x_ref, xn_ref, gin_ref[...])
    acc[...] = x_ref[0]

    def up(c):
        g_buf[c % 2] = jnp.dot(hext[...], wg_ref[c], preferred_element_type=F32)
        u_buf[c % 2] = jnp.dot(hext[HALO:HALO + tm], wu_ref[c], preferred_element_type=F32)

    def down(c):
        g = g_buf[c % 2]
        cw = cw_ref[c]
        gc = (_shift_rows(g, -1, tm) * cw[0:1] + _shift_rows(g, 0, tm) * cw[1:2]
              + _shift_rows(g, 1, tm) * cw[2:3] + cw[3:4])
        act = (gc * jax.nn.sigmoid(gc) * u_buf[c % 2]).astype(BF16)
        acc[...] += jnp.dot(act, wd_ref[c], preferred_element_type=F32)

    up(0)
    for c in range(nc):
        if c + 1 < nc:
            up(c + 1)
        down(c)
    o_ref[0] = acc[...]


def _conv_ffn(x, gin, wg, wu, cw, wd):
    B, S, D = x.shape
    nc, _, fc = wg.shape
    tm = min(512, S)
    return pl.pallas_call(
        _ffn_kernel,
        grid=(B, S // tm),
        in_specs=_halo_specs(S, tm, D) + [
            _resident((1, D)),
            _resident((nc, D, fc)),
            _resident((nc, D, fc)),
            _resident((nc, 8, fc)),
            _resident((nc, fc, D)),
        ],
        out_specs=pl.BlockSpec((1, tm, D), lambda b, i: (b, i, 0)),
        out_shape=jax.ShapeDtypeStruct((B, S, D), F32),
        scratch_shapes=[pltpu.VMEM((tm + 2 * HALO, D), BF16), pltpu.VMEM((2, tm + 2 * HALO, fc), F32),
                        pltpu.VMEM((2, tm, fc), F32), pltpu.VMEM((tm, D), F32)],
        compiler_params=_cparams(("parallel", "parallel")),
        name="conv_ffn",
    )(x, x, x, gin, wg, wu, cw, wd)


def _rgin_kernel(xp_ref, x_ref, xn_ref, gin_ref, wg_ref, wu_ref, cw_ref, gate_ref, u_ref, hext, *, cn):
    tm = x_ref.shape[1]
    _fill_hext(hext, xp_ref, x_ref, xn_ref, gin_ref[...])
    n_out = wg_ref.shape[1]
    for j in range(n_out // cn):
        cols = slice(j * cn, (j + 1) * cn)
        gate_ref[0, :, cols] = jnp.dot(hext[HALO:HALO + tm], wg_ref[:, cols], preferred_element_type=F32)
        u = jnp.dot(hext[...], wu_ref[:, cols], preferred_element_type=F32)
        cw = cw_ref[:, cols]
        u_ref[0, :, cols] = (_shift_rows(u, -2, tm) * cw[0:1] + _shift_rows(u, -1, tm) * cw[1:2]
                             + _shift_rows(u, 0, tm) * cw[2:3] + _shift_rows(u, 1, tm) * cw[3:4] + cw[4:5])


def _rg_in(x, gin, wg, wu, cw):
    B, S, D = x.shape
    C = wg.shape[1]
    tm = min(512, S)
    kern = functools.partial(_rgin_kernel, cn=512)
    out = jax.ShapeDtypeStruct((B, S, C), F32)
    ospec = pl.BlockSpec((1, tm, C), lambda b, i: (b, i, 0))
    return pl.pallas_call(
        kern,
        grid=(B, S // tm),
        in_specs=_halo_specs(S, tm, D) + [
            _resident((1, D)), _resident((D, C)), _resident((D, C)), _resident((8, C))],
        out_specs=[ospec, ospec],
        out_shape=[out, out],
        scratch_shapes=[pltpu.VMEM((tm + 2 * HALO, D), BF16)],
        compiler_params=_cparams(("parallel", "parallel")),
        name="rg_in",
    )(x, x, x, gin, wg, wu, cw)


def _rg_gate_tiles(C, bw):
    tiles = []
    for j in range(C // MXU_DIM):
        b_lo = (j * MXU_DIM) // bw
        b_hi = (j * MXU_DIM + MXU_DIM - 1) // bw
        k_lo = (b_lo * bw) // LANES * LANES
        k_hi = min(C, -(-((b_hi + 1) * bw) // LANES) * LANES)
        tiles.append((k_lo, k_hi))
    return tiles


def _rgscan_kernel(*refs, reverse, final, bw):
    if final:
        (u_ref, wr_ref, wi_ref, gb_ref, hf_ref, gate_ref, x_ref, wo_ref, o_ref,
         a_s, b_s, h_s, carry) = refs
    else:
        u_ref, wr_ref, wi_ref, gb_ref, o_ref, a_s, b_s, h_s, carry = refs
    tc, C = u_ref.shape[1], u_ref.shape[2]

    @pl.when(pl.program_id(1) == 0)
    def _init():
        carry[...] = jnp.zeros(carry.shape, F32)

    u = u_ref[0]
    ub = u.astype(BF16)
    for j, (k_lo, k_hi) in enumerate(_rg_gate_tiles(C, bw)):
        cols = slice(j * MXU_DIM, (j + 1) * MXU_DIM)
        rp = jnp.dot(ub[:, k_lo:k_hi], wr_ref[k_lo:k_hi, cols], preferred_element_type=F32) + gb_ref[0:1, cols]
        ip = jnp.dot(ub[:, k_lo:k_hi], wi_ref[k_lo:k_hi, cols], preferred_element_type=F32) + gb_ref[1:2, cols]
        log_a = (-RG_C * jax.nn.softplus(-gb_ref[2:3, cols])) * jax.nn.sigmoid(rp)
        a = jnp.exp(log_a)
        a_s[:, cols] = a
        b_s[:, cols] = jnp.sqrt(-jnp.tanh(log_a) * (a * a + 1.0)) * (jax.nn.sigmoid(ip) * u[:, cols])

    ng = tc // 8

    def body(gi, h):
        g = ng - 1 - gi if reverse else gi
        r0 = pl.multiple_of(g * 8, 8)
        a8 = a_s[pl.ds(r0, 8), :]
        b8 = b_s[pl.ds(r0, 8), :]
        rows = [None] * 8
        for t in (range(7, -1, -1) if reverse else range(8)):
            h = a8[t:t + 1] * h + b8[t:t + 1]
            rows[t] = h
        h_s[pl.ds(r0, 8), :] = jnp.concatenate(rows, axis=0)
        return h

    carry[0:1, :] = lax.fori_loop(0, ng, body, carry[0:1, :])
    if final:
        y = ((hf_ref[0] + h_s[...]) * jax.nn.gelu(gate_ref[0])).astype(BF16)
        o_ref[0] = x_ref[0] + jnp.dot(y, wo_ref[...], preferred_element_type=F32)
    else:
        o_ref[0] = h_s[...]


def _rg_scan(u, wr, wi, gb, *, reverse, extra=None):
    B, S, C = u.shape
    tc = min(256, S)
    n = S // tc
    idx = (lambda b, i: (b, n - 1 - i, 0)) if reverse else (lambda b, i: (b, i, 0))
    final = extra is not None
    in_specs = [pl.BlockSpec((1, tc, C), idx), _resident((C, C)), _resident((C, C)), _resident((8, C))]
    args = [u, wr, wi, gb]
    out_w = C
    if final:
        hf, gate, x, wo = extra
        out_w = x.shape[2]
        in_specs += [pl.BlockSpec((1, tc, C), idx), pl.BlockSpec((1, tc, C), idx),
                     pl.BlockSpec((1, tc, out_w), idx), _resident(wo.shape)]
        args += [hf, gate, x, wo]
    kern = functools.partial(_rgscan_kernel, reverse=reverse, final=final, bw=C // RG_BLOCKS)
    return pl.pallas_call(
        kern,
        grid=(B, n),
        in_specs=in_specs,
        out_specs=pl.BlockSpec((1, tc, out_w), idx),
        out_shape=jax.ShapeDtypeStruct((B, S, out_w), F32),
        scratch_shapes=[pltpu.VMEM((tc, C), F32), pltpu.VMEM((tc, C), F32), pltpu.VMEM((tc, C), F32),
                        pltpu.VMEM((8, C), F32)],
        compiler_params=_cparams(("parallel", "arbitrary")),
        name="rg_scan_final" if final else "rg_scan",
    )(*args)


def _rope_tables(S, dh):
    rot = dh // ROT_FRAC
    half = rot // 2
    inv = ROPE_THETA ** (-jnp.arange(half, dtype=F32) * 2.0 / rot)
    ang = jnp.arange(S, dtype=F32)[:, None] * inv[None, :]
    cos, sin = jnp.cos(ang), jnp.sin(ang)
    rest = dh - rot
    c = jnp.concatenate([cos, cos, jnp.ones((S, rest), F32)], axis=1)
    s = jnp.concatenate([-sin, sin, jnp.zeros((S, rest), F32)], axis=1)
    reps = LANES // dh
    return jnp.tile(c, (1, reps)), jnp.tile(s, (1, reps))


def _pad_rows(a, rows=8):
    return jnp.pad(a, ((0, rows - a.shape[0]), (0, 0)))


def _block_diag(w):
    n, c, e = w.shape
    eye = jnp.eye(n, dtype=w.dtype)
    return (w[:, :, None, :] * eye[:, None, :, None]).reshape(n * c, n * e)


def _prepare(p):
    depth, D = p['norm_mix'].shape
    F = p['ffn_w_down'].shape[1]
    fc = MXU_DIM
    nc = F // fc
    prep = {'norm_mix': p['norm_mix'].reshape(depth, 1, D), 'norm_ffn': p['norm_ffn'].reshape(depth, 1, D)}
    wup = p['ffn_w_up'].astype(BF16)
    prep['ffn_wg'] = wup[:, :, :F].reshape(depth, D, nc, fc).transpose(0, 2, 1, 3)
    prep['ffn_wu'] = wup[:, :, F:].reshape(depth, D, nc, fc).transpose(0, 2, 1, 3)
    cw = jnp.concatenate([p['ffn_conv_w'], p['ffn_conv_b'][:, None, :]], axis=1)
    cw = jnp.pad(cw, ((0, 0), (0, 8 - cw.shape[1]), (0, 0)))
    prep['ffn_cw'] = cw.reshape(depth, 8, nc, fc).transpose(0, 2, 1, 3)
    prep['ffn_wd'] = p['ffn_w_down'].astype(BF16).reshape(depth, nc, fc, D)

    n_da_q = DA_HEADS * 2
    n_qk = 2 * n_da_q * DA_DK
    prep['da_w_qk'] = p['da_w_qkv'][:, :, :n_qk].astype(BF16)
    prep['da_w_vt'] = p['da_w_qkv'][:, :, n_qk:].transpose(0, 2, 1).astype(BF16)
    prep['da_geff'] = jnp.concatenate([jnp.tile(p['da_q_norm'] * (DA_DK ** -0.5 * LOG2E), (1, n_da_q)),
                                       jnp.tile(p['da_k_norm'], (1, n_da_q))], axis=1)[:, None, :]
    prep['da_w_o'] = p['da_w_o'].astype(BF16)
    for name in ('da_lambda_q1', 'da_lambda_k1', 'da_lambda_q2', 'da_lambda_k2', 'da_sub_norm'):
        prep[name] = p[name][:, None, :]

    qw, kvw = WA_HEADS * WA_DH, WA_KV_HEADS * WA_DH
    w = p['wa_w_qkv']
    n_b = w.shape[0]

    def dup(cols):
        c = cols.reshape(n_b, D, WA_KV_HEADS, 1, WA_DH)
        return jnp.broadcast_to(c, (n_b, D, WA_KV_HEADS, 2, WA_DH)).reshape(n_b, D, 2 * kvw)

    prep['wa_w_qkv'] = jnp.concatenate(
        [w[:, :, :qw], dup(w[:, :, qw:qw + kvw]), dup(w[:, :, qw + kvw:])], axis=2).astype(BF16)
    prep['wa_geff'] = jnp.concatenate([jnp.tile(p['wa_q_norm'] * (WA_DH ** -0.5), (1, WA_HEADS)),
                                       jnp.tile(p['wa_k_norm'], (1, 2 * WA_KV_HEADS))], axis=1)[:, None, :]
    prep['wa_sink'] = p['wa_sink']
    prep['wa_w_o'] = p['wa_w_o'].astype(BF16)

    C = p['rg_w_out'].shape[1]
    bw = C // RG_BLOCKS
    win = p['rg_w_in'].astype(BF16)
    prep['rg_wg'], prep['rg_wu'] = win[:, :, :C], win[:, :, C:]
    rcw = jnp.concatenate([p['rg_conv_w'], p['rg_conv_b'][:, None, :]], axis=1)
    prep['rg_cw'] = jnp.pad(rcw, ((0, 0), (0, 8 - rcw.shape[1]), (0, 0)))
    gw, gb = p['rg_gate_w'], p['rg_gate_b']
    n_c = gw.shape[0]
    prep['rg_wr'] = jnp.stack([jnp.stack([_block_diag(gw[j, d, :, :, :bw]) for d in range(2)])
                               for j in range(n_c)]).astype(BF16)
    prep['rg_wi'] = jnp.stack([jnp.stack([_block_diag(gw[j, d, :, :, bw:]) for d in range(2)])
                               for j in range(n_c)]).astype(BF16)
    gvec = jnp.stack([gb[..., :bw].reshape(n_c, 2, C), gb[..., bw:].reshape(n_c, 2, C), p['rg_lambda']], axis=2)
    prep['rg_gb'] = jnp.pad(gvec, ((0, 0), (0, 0), (0, 5), (0, 0)))
    prep['rg_w_out'] = p['rg_w_out'].astype(BF16)
    return prep


def _lambda_init(layer_idx):
    return 0.8 - 0.6 * math.exp(-0.3 * layer_idx)


def _trunk(x, w):
    S = x.shape[1]
    cos, sin = _rope_tables(S, DA_DK)
    depth = w['norm_mix'].shape[0]
    for i in range(depth):
        kind, j = i % N_MIXERS, i // N_MIXERS
        gin = w['norm_mix'][i]
        if kind == 0:
            qk, vt = _project(x, gin, w['da_w_qk'][j], w['da_geff'][j], cos, sin, 2 * DA_HEADS * LANES,
                              wvt=w['da_w_vt'][j])
            o = _diff_attention(qk, vt, w['da_lambda_q1'][j], w['da_lambda_k1'][j], w['da_lambda_q2'][j],
                                w['da_lambda_k2'][j], w['da_sub_norm'][j], _lambda_init(i))
            x = _outproj(o, w['da_w_o'][j], x)
        elif kind == 1:
            n_prep = WA_HEADS * WA_DH + WA_KV_HEADS * LANES
            qkv = _project(x, gin, w['wa_w_qkv'][j], w['wa_geff'][j], cos, sin, n_prep)
            o = _window_attention(qkv, w['wa_sink'][j])
            x = _outproj(o, w['wa_w_o'][j], x)
        else:
            gate, u = _rg_in(x, gin, w['rg_wg'][j], w['rg_wu'][j], w['rg_cw'][j])
            hf = _rg_scan(u, w['rg_wr'][j, 0], w['rg_wi'][j, 0], w['rg_gb'][j, 0], reverse=False)
            x = _rg_scan(u, w['rg_wr'][j, 1], w['rg_wi'][j, 1], w['rg_gb'][j, 1], reverse=True,
                         extra=(hf, gate, x, w['rg_w_out'][j]))
        x = _conv_ffn(x, w['norm_ffn'][i], w['ffn_wg'][i], w['ffn_wu'][i], w['ffn_cw'][i], w['ffn_wd'][i])
    return x


def kernel(x_prompt, x_sample, norm_mix, norm_ffn, ffn_w_up, ffn_conv_w, ffn_conv_b, ffn_w_down, da_w_qkv, da_q_norm, da_k_norm, da_lambda_q1, da_lambda_k1, da_lambda_q2, da_lambda_k2, da_sub_norm, da_w_o, wa_w_qkv, wa_q_norm, wa_k_norm, wa_sink, wa_w_o, rg_w_in, rg_conv_w, rg_conv_b, rg_gate_w, rg_gate_b, rg_lambda, rg_w_out):
    w = _prepare(dict(
        norm_mix=norm_mix, norm_ffn=norm_ffn, ffn_w_up=ffn_w_up, ffn_conv_w=ffn_conv_w, ffn_conv_b=ffn_conv_b,
        ffn_w_down=ffn_w_down, da_w_qkv=da_w_qkv, da_q_norm=da_q_norm, da_k_norm=da_k_norm,
        da_lambda_q1=da_lambda_q1, da_lambda_k1=da_lambda_k1, da_lambda_q2=da_lambda_q2,
        da_lambda_k2=da_lambda_k2, da_sub_norm=da_sub_norm, da_w_o=da_w_o, wa_w_qkv=wa_w_qkv,
        wa_q_norm=wa_q_norm, wa_k_norm=wa_k_norm, wa_sink=wa_sink, wa_w_o=wa_w_o, rg_w_in=rg_w_in,
        rg_conv_w=rg_conv_w, rg_conv_b=rg_conv_b, rg_gate_w=rg_gate_w, rg_gate_b=rg_gate_b,
        rg_lambda=rg_lambda, rg_w_out=rg_w_out))
    return (_trunk(x_prompt, w), _trunk(x_sample, w))
```

```python
import functools
import math

import jax
import jax.numpy as jnp
from jax import lax
from jax.experimental import pallas as pl
from jax.experimental.pallas import tpu as pltpu

F32 = jnp.float32
BF16 = jnp.bfloat16

DA_HEADS = 8
DA_DK = 64
WA_HEADS = 16
WA_KV_HEADS = 4
WA_DH = 64
WINDOW = 128
RG_BLOCKS = 16
RG_C = 8.0
ROPE_THETA = 500000.0
ROT_FRAC = 4
EPS = 1e-6
N_MIXERS = 3
LOG2E = math.log2(math.e)

LANES = 128
MXU_DIM = 256
HALO = 16
ROW_TILE = 512
DA_TQ, DA_TK = 1024, 512
WA_TQ = 256
RG_TC = 256
VMEM_LIMIT = 56 * 1024 * 1024


def _cparams(sem):
    return pltpu.CompilerParams(dimension_semantics=sem, vmem_limit_bytes=VMEM_LIMIT)


def _rmsnorm_bf16(x, g):
    return (x * lax.rsqrt(jnp.mean(x * x, axis=-1, keepdims=True) + EPS) * g).astype(BF16)


def _sigmoid(x):
    return 0.5 * jnp.tanh(0.5 * x) + 0.5


def _resident(shape):
    nd = len(shape)
    return pl.BlockSpec(shape, lambda *_: (0,) * nd, pipeline_mode=pl.Buffered(1))


def _proj_kernel(x_ref, gin_ref, w_ref, geff_ref, cos_ref, sin_ref, o_ref, *, n_prep, cn):
    tm = x_ref.shape[1]
    n_out = w_ref.shape[1]
    h = _rmsnorm_bf16(x_ref[0], gin_ref[...])
    lane = lax.broadcasted_iota(jnp.int32, (tm, LANES), 1)
    lo = lane < 64
    first = (lane & 63) < 8
    cos = cos_ref[...]
    sin = sin_ref[...]
    for j in range(n_out // cn):
        y = jnp.dot(h, w_ref[:, j * cn:(j + 1) * cn], preferred_element_type=F32)
        for t in range(cn // LANES):
            col = j * cn + t * LANES
            yt = y[:, t * LANES:(t + 1) * LANES]
            if col < n_prep:
                sq = yt * yt
                s_lo = jnp.sum(jnp.where(lo, sq, 0.0), axis=-1, keepdims=True)
                s_hi = jnp.sum(jnp.where(lo, 0.0, sq), axis=-1, keepdims=True)
                r = lax.rsqrt(jnp.where(lo, s_lo, s_hi) * (1.0 / 64.0) + EPS)
                yn = yt * r * geff_ref[:, col:col + LANES]
                partner = jnp.where(first, pltpu.roll(yn, LANES - 8, 1), pltpu.roll(yn, 8, 1))
                yt = yn * cos + partner * sin
            o_ref[0, :, col:col + LANES] = yt.astype(BF16)


def _project(x, gin, w, geff, cos, sin, n_prep):
    B, S, D = x.shape
    n_out = w.shape[1]
    tm = min(ROW_TILE, S)
    kern = functools.partial(_proj_kernel, n_prep=n_prep, cn=512)
    return pl.pallas_call(
        kern,
        grid=(B, S // tm),
        in_specs=[
            pl.BlockSpec((1, tm, D), lambda b, i: (b, i, 0)),
            _resident((1, D)),
            _resident((D, n_out)),
            _resident((1, n_prep)),
            pl.BlockSpec((tm, LANES), lambda b, i: (i, 0)),
            pl.BlockSpec((tm, LANES), lambda b, i: (i, 0)),
        ],
        out_specs=pl.BlockSpec((1, tm, n_out), lambda b, i: (b, i, 0)),
        out_shape=jax.ShapeDtypeStruct((B, S, n_out), BF16),
        compiler_params=_cparams(("parallel", "parallel")),
        name="proj",
    )(x, gin, w, geff, cos, sin)


def _outproj_kernel(a_ref, w_ref, x_ref, o_ref):
    o_ref[0] = x_ref[0] + jnp.dot(a_ref[0], w_ref[...], preferred_element_type=F32)


def _outproj(a, w, x):
    B, S, K = a.shape
    D = w.shape[1]
    tm = min(ROW_TILE, S)
    return pl.pallas_call(
        _outproj_kernel,
        grid=(B, S // tm),
        in_specs=[
            pl.BlockSpec((1, tm, K), lambda b, i: (b, i, 0)),
            _resident((K, D)),
            pl.BlockSpec((1, tm, D), lambda b, i: (b, i, 0)),
        ],
        out_specs=pl.BlockSpec((1, tm, D), lambda b, i: (b, i, 0)),
        out_shape=jax.ShapeDtypeStruct((B, S, D), F32),
        compiler_params=_cparams(("parallel", "parallel")),
        name="outproj",
    )(a, w, x)


def _da_kernel(q_ref, k_ref, v_ref, lq1_ref, lk1_ref, lq2_ref, lk2_ref, subg_ref, o_ref,
               qs_ref, vext_ref, s_buf, mx_buf, p_buf, m_ref, acc_ref, *, lambda_init, tk):
    tq = q_ref.shape[1]
    n_units = k_ref.shape[1] // tk
    dv = LANES

    @pl.when(pl.program_id(2) == 0)
    def _widen_v():
        lane = lax.broadcasted_iota(jnp.int32, v_ref.shape[1:], 1)
        vext_ref[:, :dv] = v_ref[0]
        vext_ref[:, dv:] = jnp.where(lane == 0, 1.0, 0.0).astype(BF16)

    q = q_ref[0]
    lane = lax.broadcasted_iota(jnp.int32, q.shape, 1)
    zero = jnp.zeros_like(q)
    qs_ref[:tq] = jnp.where(lane < DA_DK, q, zero)
    qs_ref[tq:] = jnp.where(lane < DA_DK, zero, q)
    m_ref[...] = jnp.full(m_ref.shape, -jnp.inf, F32)
    acc_ref[...] = jnp.zeros(acc_ref.shape, F32)

    def scores(t, slot):
        kb = k_ref[0, pl.ds(pl.multiple_of(t * tk, tk), tk), :]
        s = lax.dot_general(qs_ref[...], kb, (((1,), (1,)), ((), ())), preferred_element_type=F32)
        s_buf[slot] = s
        mx_buf[slot] = jnp.broadcast_to(jnp.max(s, axis=-1, keepdims=True), mx_buf.shape[1:])

    def softmax_pv(t, slot):
        m_prev = m_ref[...]
        m_new = jnp.maximum(m_prev, mx_buf[slot])
        alpha = jnp.exp2(m_prev - m_new)
        p_buf[slot] = jnp.exp2(s_buf[slot] - jnp.concatenate([m_new] * (tk // LANES), axis=1)).astype(BF16)
        vb = vext_ref[pl.ds(pl.multiple_of(t * tk, tk), tk), :]
        acc_ref[...] = (jnp.concatenate([alpha, alpha], axis=1) * acc_ref[...]
                        + jnp.dot(p_buf[slot], vb, preferred_element_type=F32))
        m_ref[...] = m_new

    scores(0, 0)
    scores(1, 1)

    def body(i, carry):
        for u in range(2):
            softmax_pv(2 * i + u, u)
            scores(2 * i + u + 2, u)
        return carry

    lax.fori_loop(0, n_units // 2 - 1, body, 0)
    softmax_pv(n_units - 2, 0)
    softmax_pv(n_units - 1, 1)

    lam = (jnp.exp(jnp.sum(lq1_ref[...] * lk1_ref[...], axis=-1, keepdims=True))
           - jnp.exp(jnp.sum(lq2_ref[...] * lk2_ref[...], axis=-1, keepdims=True)) + lambda_init)
    a0, a1 = acc_ref[:tq], acc_ref[tq:]
    o = a0[:, :dv] / a0[:, dv:dv + 1] - lam * (a1[:, :dv] / a1[:, dv:dv + 1])
    o = o * lax.rsqrt(jnp.mean(o * o, axis=-1, keepdims=True) + EPS)
    o_ref[0] = (o * (subg_ref[...] * (1.0 - lambda_init))).astype(BF16)


def _diff_attention(qkv, lq1, lk1, lq2, lk2, subg, lambda_init):
    B, S, _ = qkv.shape
    H = DA_HEADS
    tq = min(DA_TQ, S)
    tk = min(DA_TK, S // 2)
    kern = functools.partial(_da_kernel, lambda_init=lambda_init, tk=tk)
    vec = _resident((1, DA_DK))
    return pl.pallas_call(
        kern,
        grid=(B, H, S // tq),
        in_specs=[
            pl.BlockSpec((1, tq, LANES), lambda b, h, i: (b, i, h)),
            pl.BlockSpec((1, S, LANES), lambda b, h, i: (b, 0, H + h)),
            pl.BlockSpec((1, S, LANES), lambda b, h, i: (b, 0, 2 * H + h)),
            vec, vec, vec, vec,
            _resident((1, LANES)),
        ],
        out_specs=pl.BlockSpec((1, tq, LANES), lambda b, h, i: (b, i, h)),
        out_shape=jax.ShapeDtypeStruct((B, S, H * LANES), BF16),
        scratch_shapes=[
            pltpu.VMEM((2 * tq, LANES), BF16),
            pltpu.VMEM((S, MXU_DIM), BF16),
            pltpu.VMEM((2, 2 * tq, tk), F32),
            pltpu.VMEM((2, 2 * tq, LANES), F32),
            pltpu.VMEM((2, 2 * tq, tk), BF16),
            pltpu.VMEM((2 * tq, LANES), F32),
            pltpu.VMEM((2 * tq, MXU_DIM), F32),
        ],
        compiler_params=_cparams(("parallel", "parallel", "arbitrary")),
        name="diff_attn",
    )(qkv, qkv, qkv, lq1, lk1, lq2, lk2, subg)


def _wa_kernel(sink_ref, q_ref, kp_ref, kc_ref, kn_ref, vp_ref, vc_ref, vn_ref, o_ref, *, seq_len):
    i = pl.program_id(1)
    tq = q_ref.shape[1]
    nk = tq + 2 * WINDOW
    kcat = jnp.concatenate([kp_ref[0], kc_ref[0], kn_ref[0]], axis=0)
    vcat = jnp.concatenate([vp_ref[0], vc_ref[0], vn_ref[0]], axis=0)
    qpos = i * tq + lax.broadcasted_iota(jnp.int32, (tq, nk), 0)
    kpos = i * tq - WINDOW + lax.broadcasted_iota(jnp.int32, (tq, nk), 1)
    valid = (kpos >= 0) & (kpos < seq_len) & (jnp.abs(qpos - kpos) <= WINDOW)
    lane = lax.broadcasted_iota(jnp.int32, (tq, LANES), 1)
    lo = lane < WA_DH
    group = WA_HEADS // WA_KV_HEADS
    for pair in range(WA_HEADS // 2):
        qt = q_ref[0, :, pair * LANES:(pair + 1) * LANES]
        zero = jnp.zeros_like(qt)
        outs = []
        for half in range(2):
            h = 2 * pair + half
            g = h // group
            qh = jnp.where(lo, qt, zero) if half == 0 else jnp.where(lo, zero, qt)
            kg = kcat[:, g * LANES:(g + 1) * LANES]
            vg = vcat[:, g * LANES:(g + 1) * LANES]
            s = lax.dot_general(qh, kg, (((1,), (1,)), ((), ())), preferred_element_type=F32)
            s = jnp.where(valid, s, -jnp.inf)
            sink = sink_ref[h]
            m = jnp.maximum(jnp.max(s, axis=-1, keepdims=True), sink)
            p = jnp.exp(s - m)
            denom = jnp.sum(p, axis=-1, keepdims=True) + jnp.exp(sink - m)
            outs.append(jnp.dot(p.astype(BF16), vg, preferred_element_type=F32) / denom)
        o_ref[0, :, pair * LANES:(pair + 1) * LANES] = jnp.where(lo, outs[0], outs[1]).astype(BF16)


def _window_attention(qkv, sink):
    B, S, _ = qkv.shape
    tq = min(WA_TQ, S)
    qw = WA_HEADS * WA_DH
    kw = WA_KV_HEADS * LANES
    qb, kb, vb = 0, qw // kw, qw // kw + 1
    r = tq // WINDOW
    last = S // WINDOW - 1
    prev = lambda b, i: (b, jnp.maximum(i * r - 1, 0))
    nxt = lambda b, i: (b, jnp.minimum((i + 1) * r, last))
    kern = functools.partial(_wa_kernel, seq_len=S)
    return pl.pallas_call(
        kern,
        grid=(B, S // tq),
        in_specs=[
            pl.BlockSpec(memory_space=pltpu.SMEM),
            pl.BlockSpec((1, tq, qw), lambda b, i: (b, i, qb)),
            pl.BlockSpec((1, WINDOW, kw), lambda b, i: prev(b, i) + (kb,)),
            pl.BlockSpec((1, tq, kw), lambda b, i: (b, i, kb)),
            pl.BlockSpec((1, WINDOW, kw), lambda b, i: nxt(b, i) + (kb,)),
            pl.BlockSpec((1, WINDOW, kw), lambda b, i: prev(b, i) + (vb,)),
            pl.BlockSpec((1, tq, kw), lambda b, i: (b, i, vb)),
            pl.BlockSpec((1, WINDOW, kw), lambda b, i: nxt(b, i) + (vb,)),
        ],
        out_specs=pl.BlockSpec((1, tq, qw), lambda b, i: (b, i, 0)),
        out_shape=jax.ShapeDtypeStruct((B, S, qw), BF16),
        compiler_params=_cparams(("parallel", "parallel")),
        name="window_attn",
    )(sink, qkv, qkv, qkv, qkv, qkv, qkv, qkv)


def _fill_hext(hext, xp_ref, x_ref, xn_ref, gin):
    i = pl.program_id(1)
    tm = x_ref.shape[1]
    hp = _rmsnorm_bf16(xp_ref[0], gin)
    hn = _rmsnorm_bf16(xn_ref[0], gin)
    hext[0:HALO] = jnp.where(i > 0, hp, jnp.zeros_like(hp))
    hext[HALO:HALO + tm] = _rmsnorm_bf16(x_ref[0], gin)
    hext[HALO + tm:] = jnp.where(i < pl.num_programs(1) - 1, hn, jnp.zeros_like(hn))


def _halo_specs(S, tm, D):
    r = tm // HALO
    last = S // HALO - 1
    return [
        pl.BlockSpec((1, HALO, D), lambda b, i: (b, jnp.maximum(i * r - 1, 0), 0)),
        pl.BlockSpec((1, tm, D), lambda b, i: (b, i, 0)),
        pl.BlockSpec((1, HALO, D), lambda b, i: (b, jnp.minimum((i + 1) * r, last), 0)),
    ]


def _shift_rows(y, off, tm):
    if off == 0:
        return y[HALO:HALO + tm]
    return pltpu.roll(y, (-off) % y.shape[0], 0)[HALO:HALO + tm]


def _ffn_kernel(xp_ref, x_ref, xn_ref, gin_ref, wg_ref, wu_ref, cw_ref, wd_ref, o_ref, hext, g_buf, u_buf, acc):
    tm = x_ref.shape[1]
    nc = wg_ref.shape[0]
    _fill_hext(hext, xp_ref, x_ref, xn_ref, gin_ref[...])
    acc[...] = x_ref[0]

    def up(c):
        g_buf[c % 2] = jnp.dot(hext[...], wg_ref[c], preferred_element_type=F32)
        u_buf[c % 2] = jnp.dot(hext[HALO:HALO + tm], wu_ref[c], preferred_element_type=F32)

    def down(c):
        g = g_buf[c % 2]
        cw = cw_ref[c]
        gc = (_shift_rows(g, -1, tm) * cw[0:1] + _shift_rows(g, 0, tm) * cw[1:2]
              + _shift_rows(g, 1, tm) * cw[2:3] + cw[3:4])
        act = (gc * jax.nn.sigmoid(gc) * u_buf[c % 2]).astype(BF16)
        acc[...] += jnp.dot(act, wd_ref[c], preferred_element_type=F32)

    up(0)
    for c in range(nc):
        if c + 1 < nc:
            up(c + 1)
        down(c)
    o_ref[0] = acc[...]


def _conv_ffn(x, gin, wg, wu, cw, wd):
    B, S, D = x.shape
    nc, _, fc = wg.shape
    tm = min(ROW_TILE, S)
    return pl.pallas_call(
        _ffn_kernel,
        grid=(B, S // tm),
        in_specs=_halo_specs(S, tm, D) + [
            _resident((1, D)),
            _resident((nc, D, fc)),
            _resident((nc, D, fc)),
            _resident((nc, 8, fc)),
            _resident((nc, fc, D)),
        ],
        out_specs=pl.BlockSpec((1, tm, D), lambda b, i: (b, i, 0)),
        out_shape=jax.ShapeDtypeStruct((B, S, D), F32),
        scratch_shapes=[pltpu.VMEM((tm + 2 * HALO, D), BF16), pltpu.VMEM((2, tm + 2 * HALO, fc), F32),
                        pltpu.VMEM((2, tm, fc), F32), pltpu.VMEM((tm, D), F32)],
        compiler_params=_cparams(("parallel", "parallel")),
        name="conv_ffn",
    )(x, x, x, gin, wg, wu, cw, wd)


def _rgin_kernel(xp_ref, x_ref, xn_ref, gin_ref, wg_ref, wu_ref, cw_ref, gate_ref, u_ref, hext, *, cn):
    tm = x_ref.shape[1]
    _fill_hext(hext, xp_ref, x_ref, xn_ref, gin_ref[...])
    n_out = wg_ref.shape[1]
    for j in range(n_out // cn):
        cols = slice(j * cn, (j + 1) * cn)
        gate_ref[0, :, cols] = jnp.dot(hext[HALO:HALO + tm], wg_ref[:, cols], preferred_element_type=F32)
        u = jnp.dot(hext[...], wu_ref[:, cols], preferred_element_type=F32)
        cw = cw_ref[:, cols]
        u_ref[0, :, cols] = (_shift_rows(u, -2, tm) * cw[0:1] + _shift_rows(u, -1, tm) * cw[1:2]
                             + _shift_rows(u, 0, tm) * cw[2:3] + _shift_rows(u, 1, tm) * cw[3:4] + cw[4:5])


def _rg_in(x, gin, wg, wu, cw):
    B, S, D = x.shape
    C = wg.shape[1]
    tm = min(ROW_TILE, S)
    kern = functools.partial(_rgin_kernel, cn=512)
    out = jax.ShapeDtypeStruct((B, S, C), F32)
    ospec = pl.BlockSpec((1, tm, C), lambda b, i: (b, i, 0))
    return pl.pallas_call(
        kern,
        grid=(B, S // tm),
        in_specs=_halo_specs(S, tm, D) + [
            _resident((1, D)), _resident((D, C)), _resident((D, C)), _resident((8, C))],
        out_specs=[ospec, ospec],
        out_shape=[out, out],
        scratch_shapes=[pltpu.VMEM((tm + 2 * HALO, D), BF16)],
        compiler_params=_cparams(("parallel", "parallel")),
        name="rg_in",
    )(x, x, x, gin, wg, wu, cw)


def _rg_gate_tiles(C, bw):
    tiles = []
    for j in range(C // MXU_DIM):
        b_lo = (j * MXU_DIM) // bw
        b_hi = (j * MXU_DIM + MXU_DIM - 1) // bw
        k_lo = (b_lo * bw) // LANES * LANES
        k_hi = min(C, -(-((b_hi + 1) * bw) // LANES) * LANES)
        tiles.append((k_lo, k_hi))
    return tiles


def _rgscan_kernel(*refs, reverse, final, bw):
    if final:
        (u_ref, wr_ref, wi_ref, gb_ref, hf_ref, gate_ref, x_ref, wo_ref, o_ref,
         a_s, b_s, h_s, carry) = refs
    else:
        u_ref, wr_ref, wi_ref, gb_ref, o_ref, a_s, b_s, h_s, carry = refs
    tc, C = u_ref.shape[1], u_ref.shape[2]

    @pl.when(pl.program_id(1) == 0)
    def _init():
        carry[...] = jnp.zeros(carry.shape, F32)

    u = u_ref[0]
    ub = u.astype(BF16)
    for j, (k_lo, k_hi) in enumerate(_rg_gate_tiles(C, bw)):
        cols = slice(j * MXU_DIM, (j + 1) * MXU_DIM)
        rp = jnp.dot(ub[:, k_lo:k_hi], wr_ref[k_lo:k_hi, cols], preferred_element_type=F32) + gb_ref[0:1, cols]
        ip = jnp.dot(ub[:, k_lo:k_hi], wi_ref[k_lo:k_hi, cols], preferred_element_type=F32) + gb_ref[1:2, cols]
        log_a = (-RG_C * jax.nn.softplus(-gb_ref[2:3, cols])) * _sigmoid(rp)
        a = jnp.exp(log_a)
        a_s[:, cols] = a
        b_s[:, cols] = jnp.sqrt(-jnp.tanh(log_a) * (a * a + 1.0)) * (_sigmoid(ip) * u[:, cols])

    ng = tc // 8

    def body(gi, h):
        g = ng - 1 - gi if reverse else gi
        r0 = pl.multiple_of(g * 8, 8)
        a8 = a_s[pl.ds(r0, 8), :]
        b8 = b_s[pl.ds(r0, 8), :]
        rows = [None] * 8
        for t in (range(7, -1, -1) if reverse else range(8)):
            h = a8[t:t + 1] * h + b8[t:t + 1]
            rows[t] = h
        h_s[pl.ds(r0, 8), :] = jnp.concatenate(rows, axis=0)
        return h

    carry[0:1, :] = lax.fori_loop(0, ng, body, carry[0:1, :])
    if final:
        y = ((hf_ref[0] + h_s[...]) * jax.nn.gelu(gate_ref[0])).astype(BF16)
        o_ref[0] = x_ref[0] + jnp.dot(y, wo_ref[...], preferred_element_type=F32)
    else:
        o_ref[0] = h_s[...]


def _rg_scan(u, wr, wi, gb, *, reverse, extra=None):
    B, S, C = u.shape
    tc = min(RG_TC, S)
    n = S // tc
    idx = (lambda b, i: (b, n - 1 - i, 0)) if reverse else (lambda b, i: (b, i, 0))
    final = extra is not None
    in_specs = [pl.BlockSpec((1, tc, C), idx), _resident((C, C)), _resident((C, C)), _resident((8, C))]
    args = [u, wr, wi, gb]
    out_w = C
    if final:
        hf, gate, x, wo = extra
        out_w = x.shape[2]
        in_specs += [pl.BlockSpec((1, tc, C), idx), pl.BlockSpec((1, tc, C), idx),
                     pl.BlockSpec((1, tc, out_w), idx), _resident(wo.shape)]
        args += [hf, gate, x, wo]
    kern = functools.partial(_rgscan_kernel, reverse=reverse, final=final, bw=C // RG_BLOCKS)
    return pl.pallas_call(
        kern,
        grid=(B, n),
        in_specs=in_specs,
        out_specs=pl.BlockSpec((1, tc, out_w), idx),
        out_shape=jax.ShapeDtypeStruct((B, S, out_w), F32),
        scratch_shapes=[pltpu.VMEM((tc, C), F32), pltpu.VMEM((tc, C), F32), pltpu.VMEM((tc, C), F32),
                        pltpu.VMEM((8, C), F32)],
        compiler_params=_cparams(("parallel", "arbitrary")),
        name="rg_scan_final" if final else "rg_scan",
    )(*args)


def _rope_tables(S, dh):
    rot = dh // ROT_FRAC
    half = rot // 2
    inv = ROPE_THETA ** (-jnp.arange(half, dtype=F32) * 2.0 / rot)
    ang = jnp.arange(S, dtype=F32)[:, None] * inv[None, :]
    cos, sin = jnp.cos(ang), jnp.sin(ang)
    rest = dh - rot
    c = jnp.concatenate([cos, cos, jnp.ones((S, rest), F32)], axis=1)
    s = jnp.concatenate([-sin, sin, jnp.zeros((S, rest), F32)], axis=1)
    reps = LANES // dh
    return jnp.tile(c, (1, reps)), jnp.tile(s, (1, reps))


def _block_diag(w):
    n, c, e = w.shape
    eye = jnp.eye(n, dtype=w.dtype)
    return (w[:, :, None, :] * eye[:, None, :, None]).reshape(n * c, n * e)


def _prepare(p):
    depth, D = p['norm_mix'].shape
    F = p['ffn_w_down'].shape[1]
    fc = MXU_DIM
    nc = F // fc
    prep = {'norm_mix': p['norm_mix'].reshape(depth, 1, D), 'norm_ffn': p['norm_ffn'].reshape(depth, 1, D)}
    wup = p['ffn_w_up'].astype(BF16)
    prep['ffn_wg'] = wup[:, :, :F].reshape(depth, D, nc, fc).transpose(0, 2, 1, 3)
    prep['ffn_wu'] = wup[:, :, F:].reshape(depth, D, nc, fc).transpose(0, 2, 1, 3)
    cw = jnp.concatenate([p['ffn_conv_w'], p['ffn_conv_b'][:, None, :]], axis=1)
    cw = jnp.pad(cw, ((0, 0), (0, 8 - cw.shape[1]), (0, 0)))
    prep['ffn_cw'] = cw.reshape(depth, 8, nc, fc).transpose(0, 2, 1, 3)
    prep['ffn_wd'] = p['ffn_w_down'].astype(BF16).reshape(depth, nc, fc, D)

    n_da_q = DA_HEADS * 2
    prep['da_w_qkv'] = p['da_w_qkv'].astype(BF16)
    prep['da_geff'] = jnp.concatenate([jnp.tile(p['da_q_norm'] * (DA_DK ** -0.5 * LOG2E), (1, n_da_q)),
                                       jnp.tile(p['da_k_norm'], (1, n_da_q))], axis=1)[:, None, :]
    prep['da_w_o'] = p['da_w_o'].astype(BF16)
    for name in ('da_lambda_q1', 'da_lambda_k1', 'da_lambda_q2', 'da_lambda_k2', 'da_sub_norm'):
        prep[name] = p[name][:, None, :]

    qw, kvw = WA_HEADS * WA_DH, WA_KV_HEADS * WA_DH
    w = p['wa_w_qkv']
    n_b = w.shape[0]

    def dup(cols):
        c = cols.reshape(n_b, D, WA_KV_HEADS, 1, WA_DH)
        return jnp.broadcast_to(c, (n_b, D, WA_KV_HEADS, 2, WA_DH)).reshape(n_b, D, 2 * kvw)

    prep['wa_w_qkv'] = jnp.concatenate(
        [w[:, :, :qw], dup(w[:, :, qw:qw + kvw]), dup(w[:, :, qw + kvw:])], axis=2).astype(BF16)
    prep['wa_geff'] = jnp.concatenate([jnp.tile(p['wa_q_norm'] * (WA_DH ** -0.5), (1, WA_HEADS)),
                                       jnp.tile(p['wa_k_norm'], (1, 2 * WA_KV_HEADS))], axis=1)[:, None, :]
    prep['wa_sink'] = p['wa_sink']
    prep['wa_w_o'] = p['wa_w_o'].astype(BF16)

    C = p['rg_w_out'].shape[1]
    bw = C // RG_BLOCKS
    win = p['rg_w_in'].astype(BF16)
    prep['rg_wg'], prep['rg_wu'] = win[:, :, :C], win[:, :, C:]
    rcw = jnp.concatenate([p['rg_conv_w'], p['rg_conv_b'][:, None, :]], axis=1)
    prep['rg_cw'] = jnp.pad(rcw, ((0, 0), (0, 8 - rcw.shape[1]), (0, 0)))
    gw, gb = p['rg_gate_w'], p['rg_gate_b']
    n_c = gw.shape[0]
    prep['rg_wr'] = jnp.stack([jnp.stack([_block_diag(gw[j, d, :, :, :bw]) for d in range(2)])
                               for j in range(n_c)]).astype(BF16)
    prep['rg_wi'] = jnp.stack([jnp.stack([_block_diag(gw[j, d, :, :, bw:]) for d in range(2)])
                               for j in range(n_c)]).astype(BF16)
    gvec = jnp.stack([gb[..., :bw].reshape(n_c, 2, C), gb[..., bw:].reshape(n_c, 2, C), p['rg_lambda']], axis=2)
    prep['rg_gb'] = jnp.pad(gvec, ((0, 0), (0, 0), (0, 5), (0, 0)))
    prep['rg_w_out'] = p['rg_w_out'].astype(BF16)
    return prep


def _lambda_init(layer_idx):
    return 0.8 - 0.6 * math.exp(-0.3 * layer_idx)


def _trunk(x, w):
    S = x.shape[1]
    cos, sin = _rope_tables(S, DA_DK)
    depth = w['norm_mix'].shape[0]
    for i in range(depth):
        kind, j = i % N_MIXERS, i // N_MIXERS
        gin = w['norm_mix'][i]
        if kind == 0:
            qkv = _project(x, gin, w['da_w_qkv'][j], w['da_geff'][j], cos, sin, 2 * DA_HEADS * LANES)
            o = _diff_attention(qkv, w['da_lambda_q1'][j], w['da_lambda_k1'][j], w['da_lambda_q2'][j],
                                w['da_lambda_k2'][j], w['da_sub_norm'][j], _lambda_init(i))
            x = _outproj(o, w['da_w_o'][j], x)
        elif kind == 1:
            n_prep = WA_HEADS * WA_DH + WA_KV_HEADS * LANES
            qkv = _project(x, gin, w['wa_w_qkv'][j], w['wa_geff'][j], cos, sin, n_prep)
            o = _window_attention(qkv, w['wa_sink'][j])
            x = _outproj(o, w['wa_w_o'][j], x)
        else:
            gate, u = _rg_in(x, gin, w['rg_wg'][j], w['rg_wu'][j], w['rg_cw'][j])
            hf = _rg_scan(u, w['rg_wr'][j, 0], w['rg_wi'][j, 0], w['rg_gb'][j, 0], reverse=False)
            x = _rg_scan(u, w['rg_wr'][j, 1], w['rg_wi'][j, 1], w['rg_gb'][j, 1], reverse=True,
                         extra=(hf, gate, x, w['rg_w_out'][j]))
        x = _conv_ffn(x, w['norm_ffn'][i], w['ffn_wg'][i], w['ffn_wu'][i], w['ffn_cw'][i], w['ffn_wd'][i])
    return x


def kernel(x_prompt, x_sample, norm_mix, norm_ffn, ffn_w_up, ffn_conv_w, ffn_conv_b, ffn_w_down, da_w_qkv, da_q_norm, da_k_norm, da_lambda_q1, da_lambda_k1, da_lambda_q2, da_lambda_k2, da_sub_norm, da_w_o, wa_w_qkv, wa_q_norm, wa_k_norm, wa_sink, wa_w_o, rg_w_in, rg_conv_w, rg_conv_b, rg_gate_w, rg_gate_b, rg_lambda, rg_w_out):
    w = _prepare(dict(
        norm_mix=norm_mix, norm_ffn=norm_ffn, ffn_w_up=ffn_w_up, ffn_conv_w=ffn_conv_w, ffn_conv_b=ffn_conv_b,
        ffn_w_down=ffn_w_down, da_w_qkv=da_w_qkv, da_q_norm=da_q_norm, da_k_norm=da_k_norm,
        da_lambda_q1=da_lambda_q1, da_lambda_k1=da_lambda_k1, da_lambda_q2=da_lambda_q2,
        da_lambda_k2=da_lambda_k2, da_sub_norm=da_sub_norm, da_w_o=da_w_o, wa_w_qkv=wa_w_qkv,
        wa_q_norm=wa_q_norm, wa_k_norm=wa_k_norm, wa_sink=wa_sink, wa_w_o=wa_w_o, rg_w_in=rg_w_in,
        rg_conv_w=rg_conv_w, rg_conv_b=rg_conv_b, rg_gate_w=rg_gate_w, rg_gate_b=rg_gate_b,
        rg_lambda=rg_lambda, rg_w_out=rg_w_out))
    return (_trunk(x_prompt, w), _trunk(x_sample, w))
```

```python
import functools
import math

import jax
import jax.numpy as jnp
from jax import lax
from jax.experimental import pallas as pl
from jax.experimental.pallas import tpu as pltpu

F32 = jnp.float32
BF16 = jnp.bfloat16

DA_HEADS = 8
DA_DK = 64
WA_HEADS = 16
WA_KV_HEADS = 4
WA_DH = 64
WINDOW = 128
RG_BLOCKS = 16
RG_C = 8.0
ROPE_THETA = 500000.0
ROT_FRAC = 4
EPS = 1e-6
N_MIXERS = 3
LOG2E = math.log2(math.e)

LANES = 128
MXU_DIM = 256
HALO = 16
ROW_TILE = 512
FFN_TILE = 1024
DA_TQ, DA_TK = 1024, 512
WA_TQ = 256
RG_TC = 256
VMEM_LIMIT = 56 * 1024 * 1024


def _cparams(sem):
    return pltpu.CompilerParams(dimension_semantics=sem, vmem_limit_bytes=VMEM_LIMIT)


def _rmsnorm_bf16(x, g):
    return (x * lax.rsqrt(jnp.mean(x * x, axis=-1, keepdims=True) + EPS) * g).astype(BF16)


def _sigmoid(x):
    return 0.5 * jnp.tanh(0.5 * x) + 0.5


def _in_group_a(lane):
    return (lane < 8) | ((lane >= 16) & (lane < 72))


def _rope_layout(a, n_tiles):
    lead = a.shape[:-1]
    head = a[..., :n_tiles * LANES].reshape(lead + (n_tiles, LANES))
    head = jnp.concatenate([head[..., 0:8], head[..., 64:72], head[..., 16:64],
                            head[..., 8:16], head[..., 72:LANES]], axis=-1)
    return jnp.concatenate([head.reshape(lead + (n_tiles * LANES,)), a[..., n_tiles * LANES:]], axis=-1)


def _resident(shape):
    nd = len(shape)
    return pl.BlockSpec(shape, lambda *_: (0,) * nd, pipeline_mode=pl.Buffered(1))


def _proj_kernel(x_ref, gin_ref, w_ref, geff_ref, cos_ref, sin_ref, o_ref, *, n_prep, cn):
    tm = x_ref.shape[1]
    n_out = w_ref.shape[1]
    h = _rmsnorm_bf16(x_ref[0], gin_ref[...])
    lo = _in_group_a(lax.broadcasted_iota(jnp.int32, (tm, LANES), 1))
    cos = cos_ref[...]
    sin = sin_ref[...]
    for j in range(n_out // cn):
        y = jnp.dot(h, w_ref[:, j * cn:(j + 1) * cn], preferred_element_type=F32)
        for t in range(cn // LANES):
            col = j * cn + t * LANES
            yt = y[:, t * LANES:(t + 1) * LANES]
            if col < n_prep:
                sq = yt * yt
                s_lo = jnp.sum(jnp.where(lo, sq, 0.0), axis=-1, keepdims=True)
                s_hi = jnp.sum(jnp.where(lo, 0.0, sq), axis=-1, keepdims=True)
                r = lax.rsqrt(jnp.where(lo, s_lo, s_hi) * (1.0 / 64.0) + EPS)
                yn = yt * r * geff_ref[:, col:col + LANES]
                yt = yn * cos + pltpu.roll(yn, LANES // 2, 1) * sin
            o_ref[0, :, col:col + LANES] = yt.astype(BF16)


def _project(x, gin, w, geff, cos, sin, n_prep):
    B, S, D = x.shape
    n_out = w.shape[1]
    tm = min(ROW_TILE, S)
    kern = functools.partial(_proj_kernel, n_prep=n_prep, cn=512)
    return pl.pallas_call(
        kern,
        grid=(B, S // tm),
        in_specs=[
            pl.BlockSpec((1, tm, D), lambda b, i: (b, i, 0)),
            _resident((1, D)),
            _resident((D, n_out)),
            _resident((1, n_prep)),
            pl.BlockSpec((tm, LANES), lambda b, i: (i, 0)),
            pl.BlockSpec((tm, LANES), lambda b, i: (i, 0)),
        ],
        out_specs=pl.BlockSpec((1, tm, n_out), lambda b, i: (b, i, 0)),
        out_shape=jax.ShapeDtypeStruct((B, S, n_out), BF16),
        compiler_params=_cparams(("parallel", "parallel")),
        name="proj",
    )(x, gin, w, geff, cos, sin)


def _outproj_kernel(a_ref, w_ref, x_ref, o_ref):
    o_ref[0] = x_ref[0] + jnp.dot(a_ref[0], w_ref[...], preferred_element_type=F32)


def _outproj(a, w, x):
    B, S, K = a.shape
    D = w.shape[1]
    tm = min(ROW_TILE, S)
    return pl.pallas_call(
        _outproj_kernel,
        grid=(B, S // tm),
        in_specs=[
            pl.BlockSpec((1, tm, K), lambda b, i: (b, i, 0)),
            _resident((K, D)),
            pl.BlockSpec((1, tm, D), lambda b, i: (b, i, 0)),
        ],
        out_specs=pl.BlockSpec((1, tm, D), lambda b, i: (b, i, 0)),
        out_shape=jax.ShapeDtypeStruct((B, S, D), F32),
        compiler_params=_cparams(("parallel", "parallel")),
        name="outproj",
    )(a, w, x)


def _da_kernel(q_ref, k_ref, v_ref, lq1_ref, lk1_ref, lq2_ref, lk2_ref, subg_ref, o_ref,
               qs_ref, vext_ref, s_buf, mx_buf, p_buf, m_ref, alpha_ref, acc_ref, *, lambda_init, tk):
    tq = q_ref.shape[1]
    n_units = k_ref.shape[1] // tk
    dv = LANES

    @pl.when(pl.program_id(2) == 0)
    def _widen_v():
        lane = lax.broadcasted_iota(jnp.int32, v_ref.shape[1:], 1)
        vext_ref[:, :dv] = v_ref[0]
        vext_ref[:, dv:] = jnp.where(lane == 0, 1.0, 0.0).astype(BF16)

    q = q_ref[0]
    map0 = _in_group_a(lax.broadcasted_iota(jnp.int32, q.shape, 1))
    zero = jnp.zeros_like(q)
    qs_ref[:tq] = jnp.where(map0, q, zero)
    qs_ref[tq:] = jnp.where(map0, zero, q)
    m_ref[...] = jnp.full(m_ref.shape, -jnp.inf, F32)
    acc_ref[...] = jnp.zeros(acc_ref.shape, F32)

    def scores(t, slot):
        kb = k_ref[0, pl.ds(pl.multiple_of(t * tk, tk), tk), :]
        s_buf[slot] = lax.dot_general(qs_ref[...], kb, (((1,), (1,)), ((), ())),
                                      preferred_element_type=F32)
        mx = s_buf[slot, :, 0:LANES]
        for c in range(1, tk // LANES):
            mx = jnp.maximum(mx, s_buf[slot, :, c * LANES:(c + 1) * LANES])
        mx_buf[slot] = jnp.broadcast_to(jnp.max(mx, axis=-1, keepdims=True), mx_buf.shape[1:])

    def softmax_pv(t, slot):
        m_new = jnp.maximum(m_ref[...], mx_buf[slot])
        alpha_ref[...] = jnp.exp2(m_ref[...] - m_new)
        m_ref[...] = m_new
        p_buf[slot] = jnp.exp2(s_buf[slot] - jnp.concatenate([m_ref[...]] * (tk // LANES), axis=1)).astype(BF16)
        vb = vext_ref[pl.ds(pl.multiple_of(t * tk, tk), tk), :]
        pv = jnp.dot(p_buf[slot], vb, preferred_element_type=F32)
        acc_ref[...] = jnp.concatenate([alpha_ref[...]] * 2, axis=1) * acc_ref[...] + pv

    scores(0, 0)
    scores(1, 1)

    def body(i, carry):
        for u in range(2):
            softmax_pv(2 * i + u, u)
            scores(2 * i + u + 2, u)
        return carry

    lax.fori_loop(0, n_units // 2 - 1, body, 0)
    softmax_pv(n_units - 2, 0)
    softmax_pv(n_units - 1, 1)

    lam = (jnp.exp(jnp.sum(lq1_ref[...] * lk1_ref[...], axis=-1, keepdims=True))
           - jnp.exp(jnp.sum(lq2_ref[...] * lk2_ref[...], axis=-1, keepdims=True)) + lambda_init)
    a0, a1 = acc_ref[:tq], acc_ref[tq:]
    o = a0[:, :dv] / a0[:, dv:dv + 1] - lam * (a1[:, :dv] / a1[:, dv:dv + 1])
    o = o * lax.rsqrt(jnp.mean(o * o, axis=-1, keepdims=True) + EPS)
    o_ref[0] = (o * (subg_ref[...] * (1.0 - lambda_init))).astype(BF16)


def _diff_attention(qkv, lq1, lk1, lq2, lk2, subg, lambda_init):
    B, S, _ = qkv.shape
    H = DA_HEADS
    tq = min(DA_TQ, S)
    tk = min(DA_TK, S // 2)
    kern = functools.partial(_da_kernel, lambda_init=lambda_init, tk=tk)
    vec = _resident((1, DA_DK))
    return pl.pallas_call(
        kern,
        grid=(B, H, S // tq),
        in_specs=[
            pl.BlockSpec((1, tq, LANES), lambda b, h, i: (b, i, h)),
            pl.BlockSpec((1, S, LANES), lambda b, h, i: (b, 0, H + h)),
            pl.BlockSpec((1, S, LANES), lambda b, h, i: (b, 0, 2 * H + h)),
            vec, vec, vec, vec,
            _resident((1, LANES)),
        ],
        out_specs=pl.BlockSpec((1, tq, LANES), lambda b, h, i: (b, i, h)),
        out_shape=jax.ShapeDtypeStruct((B, S, H * LANES), BF16),
        scratch_shapes=[
            pltpu.VMEM((2 * tq, LANES), BF16),
            pltpu.VMEM((S, MXU_DIM), BF16),
            pltpu.VMEM((2, 2 * tq, tk), F32),
            pltpu.VMEM((2, 2 * tq, LANES), F32),
            pltpu.VMEM((2, 2 * tq, tk), BF16),
            pltpu.VMEM((2 * tq, LANES), F32),
            pltpu.VMEM((2 * tq, LANES), F32),
            pltpu.VMEM((2 * tq, MXU_DIM), F32),
        ],
        compiler_params=_cparams(("parallel", "parallel", "arbitrary")),
        name="diff_attn",
    )(qkv, qkv, qkv, lq1, lk1, lq2, lk2, subg)


def _wa_kernel(sink_ref, q_ref, kp_ref, kc_ref, kn_ref, vp_ref, vc_ref, vn_ref, o_ref, *, seq_len):
    i = pl.program_id(1)
    tq = q_ref.shape[1]
    nk = tq + 2 * WINDOW
    kcat = jnp.concatenate([kp_ref[0], kc_ref[0], kn_ref[0]], axis=0)
    vcat = jnp.concatenate([vp_ref[0], vc_ref[0], vn_ref[0]], axis=0)
    qpos = i * tq + lax.broadcasted_iota(jnp.int32, (tq, nk), 0)
    kpos = i * tq - WINDOW + lax.broadcasted_iota(jnp.int32, (tq, nk), 1)
    valid = (kpos >= 0) & (kpos < seq_len) & (jnp.abs(qpos - kpos) <= WINDOW)
    lane = lax.broadcasted_iota(jnp.int32, (tq, LANES), 1)
    lo = lane < WA_DH
    even = _in_group_a(lane)
    group = WA_HEADS // WA_KV_HEADS
    for pair in range(WA_HEADS // 2):
        qt = q_ref[0, :, pair * LANES:(pair + 1) * LANES]
        zero = jnp.zeros_like(qt)
        outs = []
        for half in range(2):
            h = 2 * pair + half
            g = h // group
            qh = jnp.where(even, qt, zero) if half == 0 else jnp.where(even, zero, qt)
            kg = kcat[:, g * LANES:(g + 1) * LANES]
            vg = vcat[:, g * LANES:(g + 1) * LANES]
            s = lax.dot_general(qh, kg, (((1,), (1,)), ((), ())), preferred_element_type=F32)
            s = jnp.where(valid, s, -jnp.inf)
            sink = sink_ref[h]
            m = jnp.maximum(jnp.max(s, axis=-1, keepdims=True), sink)
            p = jnp.exp2(s - m)
            denom = jnp.sum(p, axis=-1, keepdims=True) + jnp.exp2(sink - m)
            outs.append(jnp.dot(p.astype(BF16), vg, preferred_element_type=F32) / denom)
        o_ref[0, :, pair * LANES:(pair + 1) * LANES] = jnp.where(lo, outs[0], outs[1]).astype(BF16)


def _window_attention(qkv, sink):
    B, S, _ = qkv.shape
    tq = min(WA_TQ, S)
    qw = WA_HEADS * WA_DH
    kw = WA_KV_HEADS * LANES
    qb, kb, vb = 0, qw // kw, qw // kw + 1
    r = tq // WINDOW
    last = S // WINDOW - 1
    prev = lambda b, i: (b, jnp.maximum(i * r - 1, 0))
    nxt = lambda b, i: (b, jnp.minimum((i + 1) * r, last))
    kern = functools.partial(_wa_kernel, seq_len=S)
    return pl.pallas_call(
        kern,
        grid=(B, S // tq),
        in_specs=[
            pl.BlockSpec(memory_space=pltpu.SMEM),
            pl.BlockSpec((1, tq, qw), lambda b, i: (b, i, qb)),
            pl.BlockSpec((1, WINDOW, kw), lambda b, i: prev(b, i) + (kb,)),
            pl.BlockSpec((1, tq, kw), lambda b, i: (b, i, kb)),
            pl.BlockSpec((1, WINDOW, kw), lambda b, i: nxt(b, i) + (kb,)),
            pl.BlockSpec((1, WINDOW, kw), lambda b, i: prev(b, i) + (vb,)),
            pl.BlockSpec((1, tq, kw), lambda b, i: (b, i, vb)),
            pl.BlockSpec((1, WINDOW, kw), lambda b, i: nxt(b, i) + (vb,)),
        ],
        out_specs=pl.BlockSpec((1, tq, qw), lambda b, i: (b, i, 0)),
        out_shape=jax.ShapeDtypeStruct((B, S, qw), BF16),
        compiler_params=_cparams(("parallel", "parallel")),
        name="window_attn",
    )(sink, qkv, qkv, qkv, qkv, qkv, qkv, qkv)


def _fill_hext(hext, xp_ref, x_ref, xn_ref, gin):
    i = pl.program_id(1)
    tm = x_ref.shape[1]
    hp = _rmsnorm_bf16(xp_ref[0], gin)
    hn = _rmsnorm_bf16(xn_ref[0], gin)
    hext[0:HALO] = jnp.where(i > 0, hp, jnp.zeros_like(hp))
    hext[HALO:HALO + tm] = _rmsnorm_bf16(x_ref[0], gin)
    hext[HALO + tm:] = jnp.where(i < pl.num_programs(1) - 1, hn, jnp.zeros_like(hn))


def _halo_specs(S, tm, D):
    r = tm // HALO
    last = S // HALO - 1
    return [
        pl.BlockSpec((1, HALO, D), lambda b, i: (b, jnp.maximum(i * r - 1, 0), 0)),
        pl.BlockSpec((1, tm, D), lambda b, i: (b, i, 0)),
        pl.BlockSpec((1, HALO, D), lambda b, i: (b, jnp.minimum((i + 1) * r, last), 0)),
    ]


def _shift_rows(y, off, tm):
    if off == 0:
        return y[HALO:HALO + tm]
    return pltpu.roll(y, (-off) % y.shape[0], 0)[HALO:HALO + tm]


def _ffn_kernel(xp_ref, x_ref, xn_ref, gin_ref, wg_ref, wu_ref, cw_ref, wd_ref, o_ref, hext, g_buf, u_buf, acc):
    tm = x_ref.shape[1]
    nc = wg_ref.shape[0]
    _fill_hext(hext, xp_ref, x_ref, xn_ref, gin_ref[...])
    acc[...] = x_ref[0]

    def up(c):
        g_buf[c % 2] = jnp.dot(hext[...], wg_ref[c], preferred_element_type=F32)
        u_buf[c % 2] = jnp.dot(hext[HALO:HALO + tm], wu_ref[c], preferred_element_type=F32)

    def down(c):
        g = g_buf[c % 2]
        cw = cw_ref[c]
        gc = (_shift_rows(g, -1, tm) * cw[0:1] + _shift_rows(g, 0, tm) * cw[1:2]
              + _shift_rows(g, 1, tm) * cw[2:3] + cw[3:4])
        act = (gc * jax.nn.sigmoid(gc) * u_buf[c % 2]).astype(BF16)
        acc[...] += jnp.dot(act, wd_ref[c], preferred_element_type=F32)

    up(0)
    for c in range(nc):
        if c + 1 < nc:
            up(c + 1)
        down(c)
    o_ref[0] = acc[...]


def _conv_ffn(x, gin, wg, wu, cw, wd):
    B, S, D = x.shape
    nc, _, fc = wg.shape
    tm = min(FFN_TILE, S)
    return pl.pallas_call(
        _ffn_kernel,
        grid=(B, S // tm),
        in_specs=_halo_specs(S, tm, D) + [
            _resident((1, D)),
            _resident((nc, D, fc)),
            _resident((nc, D, fc)),
            _resident((nc, 8, fc)),
            _resident((nc, fc, D)),
        ],
        out_specs=pl.BlockSpec((1, tm, D), lambda b, i: (b, i, 0)),
        out_shape=jax.ShapeDtypeStruct((B, S, D), F32),
        scratch_shapes=[pltpu.VMEM((tm + 2 * HALO, D), BF16), pltpu.VMEM((2, tm + 2 * HALO, fc), F32),
                        pltpu.VMEM((2, tm, fc), F32), pltpu.VMEM((tm, D), F32)],
        compiler_params=_cparams(("parallel", "parallel")),
        name="conv_ffn",
    )(x, x, x, gin, wg, wu, cw, wd)


def _rgin_kernel(xp_ref, x_ref, xn_ref, gin_ref, wg_ref, wu_ref, cw_ref, gate_ref, u_ref, hext, *, cn):
    tm = x_ref.shape[1]
    _fill_hext(hext, xp_ref, x_ref, xn_ref, gin_ref[...])
    n_out = wg_ref.shape[1]
    for j in range(n_out // cn):
        cols = slice(j * cn, (j + 1) * cn)
        gate_ref[0, :, cols] = jnp.dot(hext[HALO:HALO + tm], wg_ref[:, cols], preferred_element_type=F32)
        u = jnp.dot(hext[...], wu_ref[:, cols], preferred_element_type=F32)
        cw = cw_ref[:, cols]
        u_ref[0, :, cols] = (_shift_rows(u, -2, tm) * cw[0:1] + _shift_rows(u, -1, tm) * cw[1:2]
                             + _shift_rows(u, 0, tm) * cw[2:3] + _shift_rows(u, 1, tm) * cw[3:4] + cw[4:5])


def _rg_in(x, gin, wg, wu, cw):
    B, S, D = x.shape
    C = wg.shape[1]
    tm = min(ROW_TILE, S)
    kern = functools.partial(_rgin_kernel, cn=512)
    out = jax.ShapeDtypeStruct((B, S, C), F32)
    ospec = pl.BlockSpec((1, tm, C), lambda b, i: (b, i, 0))
    return pl.pallas_call(
        kern,
        grid=(B, S // tm),
        in_specs=_halo_specs(S, tm, D) + [
            _resident((1, D)), _resident((D, C)), _resident((D, C)), _resident((8, C))],
        out_specs=[ospec, ospec],
        out_shape=[out, out],
        scratch_shapes=[pltpu.VMEM((tm + 2 * HALO, D), BF16)],
        compiler_params=_cparams(("parallel", "parallel")),
        name="rg_in",
    )(x, x, x, gin, wg, wu, cw)


def _rg_gate_tiles(C, bw):
    tiles = []
    for j in range(C // MXU_DIM):
        b_lo = (j * MXU_DIM) // bw
        b_hi = (j * MXU_DIM + MXU_DIM - 1) // bw
        k_lo = (b_lo * bw) // LANES * LANES
        k_hi = min(C, -(-((b_hi + 1) * bw) // LANES) * LANES)
        tiles.append((k_lo, k_hi))
    return tiles


def _rgscan_kernel(*refs, reverse, final, bw):
    if final:
        (u_ref, wr_ref, wi_ref, gb_ref, hf_ref, gate_ref, x_ref, wo_ref, o_ref,
         a_s, b_s, h_s, carry) = refs
    else:
        u_ref, wr_ref, wi_ref, gb_ref, o_ref, a_s, b_s, h_s, carry = refs
    tc, C = u_ref.shape[1], u_ref.shape[2]

    @pl.when(pl.program_id(1) == 0)
    def _init():
        carry[...] = jnp.zeros(carry.shape, F32)

    u = u_ref[0]
    ub = u.astype(BF16)
    for j, (k_lo, k_hi) in enumerate(_rg_gate_tiles(C, bw)):
        cols = slice(j * MXU_DIM, (j + 1) * MXU_DIM)
        rp = jnp.dot(ub[:, k_lo:k_hi], wr_ref[k_lo:k_hi, cols], preferred_element_type=F32) + gb_ref[0:1, cols]
        ip = jnp.dot(ub[:, k_lo:k_hi], wi_ref[k_lo:k_hi, cols], preferred_element_type=F32) + gb_ref[1:2, cols]
        log_a = (-RG_C * jax.nn.softplus(-gb_ref[2:3, cols])) * _sigmoid(rp)
        a = jnp.exp(log_a)
        a_s[:, cols] = a
        b_s[:, cols] = jnp.sqrt(-jnp.tanh(log_a) * (a * a + 1.0)) * (_sigmoid(ip) * u[:, cols])

    ng = tc // 8

    def body(gi, h):
        g = ng - 1 - gi if reverse else gi
        r0 = pl.multiple_of(g * 8, 8)
        a8 = a_s[pl.ds(r0, 8), :]
        b8 = b_s[pl.ds(r0, 8), :]
        rows = [None] * 8
        for t in (range(7, -1, -1) if reverse else range(8)):
            h = a8[t:t + 1] * h + b8[t:t + 1]
            rows[t] = h
        h_s[pl.ds(r0, 8), :] = jnp.concatenate(rows, axis=0)
        return h

    carry[0:1, :] = lax.fori_loop(0, ng, body, carry[0:1, :])
    if final:
        y = ((hf_ref[0] + h_s[...]) * jax.nn.gelu(gate_ref[0])).astype(BF16)
        o_ref[0] = x_ref[0] + jnp.dot(y, wo_ref[...], preferred_element_type=F32)
    else:
        o_ref[0] = h_s[...]


def _rg_scan(u, wr, wi, gb, *, reverse, extra=None):
    B, S, C = u.shape
    tc = min(RG_TC, S)
    n = S // tc
    idx = (lambda b, i: (b, n - 1 - i, 0)) if reverse else (lambda b, i: (b, i, 0))
    final = extra is not None
    in_specs = [pl.BlockSpec((1, tc, C), idx), _resident((C, C)), _resident((C, C)), _resident((8, C))]
    args = [u, wr, wi, gb]
    out_w = C
    if final:
        hf, gate, x, wo = extra
        out_w = x.shape[2]
        in_specs += [pl.BlockSpec((1, tc, C), idx), pl.BlockSpec((1, tc, C), idx),
                     pl.BlockSpec((1, tc, out_w), idx), _resident(wo.shape)]
        args += [hf, gate, x, wo]
    kern = functools.partial(_rgscan_kernel, reverse=reverse, final=final, bw=C // RG_BLOCKS)
    return pl.pallas_call(
        kern,
        grid=(B, n),
        in_specs=in_specs,
        out_specs=pl.BlockSpec((1, tc, out_w), idx),
        out_shape=jax.ShapeDtypeStruct((B, S, out_w), F32),
        scratch_shapes=[pltpu.VMEM((tc, C), F32), pltpu.VMEM((tc, C), F32), pltpu.VMEM((tc, C), F32),
                        pltpu.VMEM((8, C), F32)],
        compiler_params=_cparams(("parallel", "arbitrary")),
        name="rg_scan_final" if final else "rg_scan",
    )(*args)


def _rope_tables(S, dh):
    rot = dh // ROT_FRAC
    half = rot // 2
    inv = ROPE_THETA ** (-jnp.arange(half, dtype=F32) * 2.0 / rot)
    ang = jnp.arange(S, dtype=F32)[:, None] * inv[None, :]
    cos, sin = jnp.cos(ang), jnp.sin(ang)
    rest = dh - rot
    c = jnp.concatenate([cos, cos, jnp.ones((S, rest), F32)], axis=1)
    s = jnp.concatenate([-sin, sin, jnp.zeros((S, rest), F32)], axis=1)
    reps = LANES // dh
    return _rope_layout(jnp.tile(c, (1, reps)), 1), _rope_layout(jnp.tile(s, (1, reps)), 1)


def _block_diag(w):
    n, c, e = w.shape
    eye = jnp.eye(n, dtype=w.dtype)
    return (w[:, :, None, :] * eye[:, None, :, None]).reshape(n * c, n * e)


def _prepare(p):
    depth, D = p['norm_mix'].shape
    F = p['ffn_w_down'].shape[1]
    fc = MXU_DIM
    nc = F // fc
    prep = {'norm_mix': p['norm_mix'].reshape(depth, 1, D), 'norm_ffn': p['norm_ffn'].reshape(depth, 1, D)}
    wup = p['ffn_w_up'].astype(BF16)
    prep['ffn_wg'] = wup[:, :, :F].reshape(depth, D, nc, fc).transpose(0, 2, 1, 3)
    prep['ffn_wu'] = wup[:, :, F:].reshape(depth, D, nc, fc).transpose(0, 2, 1, 3)
    cw = jnp.concatenate([p['ffn_conv_w'], p['ffn_conv_b'][:, None, :]], axis=1)
    cw = jnp.pad(cw, ((0, 0), (0, 8 - cw.shape[1]), (0, 0)))
    prep['ffn_cw'] = cw.reshape(depth, 8, nc, fc).transpose(0, 2, 1, 3)
    prep['ffn_wd'] = p['ffn_w_down'].astype(BF16).reshape(depth, nc, fc, D)

    n_da_q = DA_HEADS * 2
    prep['da_w_qkv'] = _rope_layout(p['da_w_qkv'], n_da_q).astype(BF16)
    da_geff = jnp.concatenate([jnp.tile(p['da_q_norm'] * (DA_DK ** -0.5 * LOG2E), (1, n_da_q)),
                               jnp.tile(p['da_k_norm'], (1, n_da_q))], axis=1)
    prep['da_geff'] = _rope_layout(da_geff, n_da_q)[:, None, :]
    prep['da_w_o'] = p['da_w_o'].astype(BF16)
    for name in ('da_lambda_q1', 'da_lambda_k1', 'da_lambda_q2', 'da_lambda_k2', 'da_sub_norm'):
        prep[name] = p[name][:, None, :]

    qw, kvw = WA_HEADS * WA_DH, WA_KV_HEADS * WA_DH
    w = p['wa_w_qkv']
    n_b = w.shape[0]

    def dup(cols):
        c = cols.reshape(n_b, D, WA_KV_HEADS, 1, WA_DH)
        return jnp.broadcast_to(c, (n_b, D, WA_KV_HEADS, 2, WA_DH)).reshape(n_b, D, 2 * kvw)

    n_wa_qk = (qw + 2 * kvw) // LANES
    wa_w = jnp.concatenate([w[:, :, :qw], dup(w[:, :, qw:qw + kvw]), dup(w[:, :, qw + kvw:])], axis=2)
    prep['wa_w_qkv'] = _rope_layout(wa_w, n_wa_qk).astype(BF16)
    wa_geff = jnp.concatenate([jnp.tile(p['wa_q_norm'] * (WA_DH ** -0.5 * LOG2E), (1, WA_HEADS)),
                               jnp.tile(p['wa_k_norm'], (1, 2 * WA_KV_HEADS))], axis=1)
    prep['wa_geff'] = _rope_layout(wa_geff, n_wa_qk)[:, None, :]
    prep['wa_sink'] = p['wa_sink'] * LOG2E
    prep['wa_w_o'] = p['wa_w_o'].astype(BF16)

    C = p['rg_w_out'].shape[1]
    bw = C // RG_BLOCKS
    win = p['rg_w_in'].astype(BF16)
    prep['rg_wg'], prep['rg_wu'] = win[:, :, :C], win[:, :, C:]
    rcw = jnp.concatenate([p['rg_conv_w'], p['rg_conv_b'][:, None, :]], axis=1)
    prep['rg_cw'] = jnp.pad(rcw, ((0, 0), (0, 8 - rcw.shape[1]), (0, 0)))
    gw, gb = p['rg_gate_w'], p['rg_gate_b']
    n_c = gw.shape[0]
    prep['rg_wr'] = jnp.stack([jnp.stack([_block_diag(gw[j, d, :, :, :bw]) for d in range(2)])
                               for j in range(n_c)]).astype(BF16)
    prep['rg_wi'] = jnp.stack([jnp.stack([_block_diag(gw[j, d, :, :, bw:]) for d in range(2)])
                               for j in range(n_c)]).astype(BF16)
    gvec = jnp.stack([gb[..., :bw].reshape(n_c, 2, C), gb[..., bw:].reshape(n_c, 2, C), p['rg_lambda']], axis=2)
    prep['rg_gb'] = jnp.pad(gvec, ((0, 0), (0, 0), (0, 5), (0, 0)))
    prep['rg_w_out'] = p['rg_w_out'].astype(BF16)
    return prep


def _lambda_init(layer_idx):
    return 0.8 - 0.6 * math.exp(-0.3 * layer_idx)


def _trunk(x, w):
    S = x.shape[1]
    cos, sin = _rope_tables(S, DA_DK)
    depth = w['norm_mix'].shape[0]
    for i in range(depth):
        kind, j = i % N_MIXERS, i // N_MIXERS
        gin = w['norm_mix'][i]
        if kind == 0:
            qkv = _project(x, gin, w['da_w_qkv'][j], w['da_geff'][j], cos, sin, 2 * DA_HEADS * LANES)
            o = _diff_attention(qkv, w['da_lambda_q1'][j], w['da_lambda_k1'][j], w['da_lambda_q2'][j],
                                w['da_lambda_k2'][j], w['da_sub_norm'][j], _lambda_init(i))
            x = _outproj(o, w['da_w_o'][j], x)
        elif kind == 1:
            n_prep = WA_HEADS * WA_DH + WA_KV_HEADS * LANES
            qkv = _project(x, gin, w['wa_w_qkv'][j], w['wa_geff'][j], cos, sin, n_prep)
            o = _window_attention(qkv, w['wa_sink'][j])
            x = _outproj(o, w['wa_w_o'][j], x)
        else:
            gate, u = _rg_in(x, gin, w['rg_wg'][j], w['rg_wu'][j], w['rg_cw'][j])
            hf = _rg_scan(u, w['rg_wr'][j, 0], w['rg_wi'][j, 0], w['rg_gb'][j, 0], reverse=False)
            x = _rg_scan(u, w['rg_wr'][j, 1], w['rg_wi'][j, 1], w['rg_gb'][j, 1], reverse=True,
                         extra=(hf, gate, x, w['rg_w_out'][j]))
        x = _conv_ffn(x, w['norm_ffn'][i], w['ffn_wg'][i], w['ffn_wu'][i], w['ffn_cw'][i], w['ffn_wd'][i])
    return x


def kernel(x_prompt, x_sample, norm_mix, norm_ffn, ffn_w_up, ffn_conv_w, ffn_conv_b, ffn_w_down, da_w_qkv, da_q_norm, da_k_norm, da_lambda_q1, da_lambda_k1, da_lambda_q2, da_lambda_k2, da_sub_norm, da_w_o, wa_w_qkv, wa_q_norm, wa_k_norm, wa_sink, wa_w_o, rg_w_in, rg_conv_w, rg_conv_b, rg_gate_w, rg_gate_b, rg_lambda, rg_w_out):
    w = _prepare(dict(
        norm_mix=norm_mix, norm_ffn=norm_ffn, ffn_w_up=ffn_w_up, ffn_conv_w=ffn_conv_w, ffn_conv_b=ffn_conv_b,
        ffn_w_down=ffn_w_down, da_w_qkv=da_w_qkv, da_q_norm=da_q_norm, da_k_norm=da_k_norm,
        da_lambda_q1=da_lambda_q1, da_lambda_k1=da_lambda_k1, da_lambda_q2=da_lambda_q2,
        da_lambda_k2=da_lambda_k2, da_sub_norm=da_sub_norm, da_w_o=da_w_o, wa_w_qkv=wa_w_qkv,
        wa_q_norm=wa_q_norm, wa_k_norm=wa_k_norm, wa_sink=wa_sink, wa_w_o=wa_w_o, rg_w_in=rg_w_in,
        rg_conv_w=rg_conv_w, rg_conv_b=rg_conv_b, rg_gate_w=rg_gate_w, rg_gate_b=rg_gate_b,
        rg_lambda=rg_lambda, rg_w_out=rg_w_out))
    return (_trunk(x_prompt, w), _trunk(x_sample, w))
```

```python
import functools
import math

import jax
import jax.numpy as jnp
from jax import lax
from jax.experimental import pallas as pl
from jax.experimental.pallas import tpu as pltpu

F32 = jnp.float32
BF16 = jnp.bfloat16

DA_HEADS = 8
DA_DK = 64
WA_HEADS = 16
WA_KV_HEADS = 4
WA_DH = 64
WINDOW = 128
RG_BLOCKS = 16
RG_C = 8.0
ROPE_THETA = 500000.0
ROT_FRAC = 4
EPS = 1e-6
N_MIXERS = 3
LOG2E = math.log2(math.e)

LANES = 128
MXU_DIM = 256
HALO = 16
ROW_TILE = 512
FFN_TILE = 1024
DA_TQ, DA_TK = 1024, 512
DA_MIN_UNITS = 8
WA_TQ = 256
RG_TC = 256
VMEM_LIMIT = 56 * 1024 * 1024


def _cparams(sem):
    return pltpu.CompilerParams(dimension_semantics=sem, vmem_limit_bytes=VMEM_LIMIT)


def _rmsnorm_bf16(x, g):
    return (x * lax.rsqrt(jnp.mean(x * x, axis=-1, keepdims=True) + EPS) * g).astype(BF16)


def _sigmoid(x):
    return 0.5 * jnp.tanh(0.5 * x) + 0.5


def _in_group_a(lane):
    return (lane < 8) | ((lane >= 16) & (lane < 72))


def _rope_layout(a, n_tiles):
    lead = a.shape[:-1]
    head = a[..., :n_tiles * LANES].reshape(lead + (n_tiles, LANES))
    head = jnp.concatenate([head[..., 0:8], head[..., 64:72], head[..., 16:64],
                            head[..., 8:16], head[..., 72:LANES]], axis=-1)
    return jnp.concatenate([head.reshape(lead + (n_tiles * LANES,)), a[..., n_tiles * LANES:]], axis=-1)


def _resident(shape):
    nd = len(shape)
    return pl.BlockSpec(shape, lambda *_: (0,) * nd, pipeline_mode=pl.Buffered(1))


def _proj_kernel(x_ref, gin_ref, w_ref, geff_ref, cos_ref, sin_ref, o_ref, *, n_prep, cn):
    tm = x_ref.shape[1]
    n_out = w_ref.shape[1]
    h = _rmsnorm_bf16(x_ref[0], gin_ref[...])
    lo = _in_group_a(lax.broadcasted_iota(jnp.int32, (tm, LANES), 1))
    cos = cos_ref[...]
    sin = sin_ref[...]
    for j in range(n_out // cn):
        y = jnp.dot(h, w_ref[:, j * cn:(j + 1) * cn], preferred_element_type=F32)
        for t in range(cn // LANES):
            col = j * cn + t * LANES
            yt = y[:, t * LANES:(t + 1) * LANES]
            if col < n_prep:
                sq = yt * yt
                s_lo = jnp.sum(jnp.where(lo, sq, 0.0), axis=-1, keepdims=True)
                s_hi = jnp.sum(jnp.where(lo, 0.0, sq), axis=-1, keepdims=True)
                r = lax.rsqrt(jnp.where(lo, s_lo, s_hi) * (1.0 / 64.0) + EPS)
                yn = yt * r * geff_ref[:, col:col + LANES]
                yt = yn * cos + pltpu.roll(yn, LANES // 2, 1) * sin
            o_ref[0, :, col:col + LANES] = yt.astype(BF16)


def _project(x, gin, w, geff, cos, sin, n_prep):
    B, S, D = x.shape
    n_out = w.shape[1]
    tm = min(ROW_TILE, S)
    kern = functools.partial(_proj_kernel, n_prep=n_prep, cn=512)
    return pl.pallas_call(
        kern,
        grid=(B, S // tm),
        in_specs=[
            pl.BlockSpec((1, tm, D), lambda b, i: (b, i, 0)),
            _resident((1, D)),
            _resident((D, n_out)),
            _resident((1, n_prep)),
            pl.BlockSpec((tm, LANES), lambda b, i: (i, 0)),
            pl.BlockSpec((tm, LANES), lambda b, i: (i, 0)),
        ],
        out_specs=pl.BlockSpec((1, tm, n_out), lambda b, i: (b, i, 0)),
        out_shape=jax.ShapeDtypeStruct((B, S, n_out), BF16),
        compiler_params=_cparams(("parallel", "parallel")),
        name="proj",
    )(x, gin, w, geff, cos, sin)


def _outproj_kernel(a_ref, w_ref, x_ref, o_ref):
    o_ref[0] = x_ref[0] + jnp.dot(a_ref[0], w_ref[...], preferred_element_type=F32)


def _outproj(a, w, x):
    B, S, K = a.shape
    D = w.shape[1]
    tm = min(ROW_TILE, S)
    return pl.pallas_call(
        _outproj_kernel,
        grid=(B, S // tm),
        in_specs=[
            pl.BlockSpec((1, tm, K), lambda b, i: (b, i, 0)),
            _resident((K, D)),
            pl.BlockSpec((1, tm, D), lambda b, i: (b, i, 0)),
        ],
        out_specs=pl.BlockSpec((1, tm, D), lambda b, i: (b, i, 0)),
        out_shape=jax.ShapeDtypeStruct((B, S, D), F32),
        compiler_params=_cparams(("parallel", "parallel")),
        name="outproj",
    )(a, w, x)


def _da_kernel(q_ref, k_ref, v_ref, lq1_ref, lk1_ref, lq2_ref, lk2_ref, subg_ref, o_ref,
               qs_ref, s_buf, mx_buf, p_buf, m_ref, alpha_ref, acc_ref, *, lambda_init, tk):
    tq = q_ref.shape[1]
    n_units = k_ref.shape[1] // tk
    dv = LANES

    q = q_ref[0]
    map0 = _in_group_a(lax.broadcasted_iota(jnp.int32, q.shape, 1))
    zero = jnp.zeros_like(q)
    qs_ref[:tq] = jnp.where(map0, q, zero)
    qs_ref[tq:] = jnp.where(map0, zero, q)
    m_ref[...] = jnp.full(m_ref.shape, -jnp.inf, F32)
    acc_ref[...] = jnp.zeros(acc_ref.shape, F32)

    def scores(t, slot):
        kb = k_ref[0, pl.ds(pl.multiple_of(t * tk, tk), tk), :]
        s_buf[slot] = lax.dot_general(qs_ref[...], kb, (((1,), (1,)), ((), ())),
                                      preferred_element_type=F32)
        mx = s_buf[slot, :, 0:LANES]
        for c in range(1, tk // LANES):
            mx = jnp.maximum(mx, s_buf[slot, :, c * LANES:(c + 1) * LANES])
        mx_buf[slot] = jnp.broadcast_to(jnp.max(mx, axis=-1, keepdims=True), mx_buf.shape[1:])

    def softmax_pv(t, slot):
        m_new = jnp.maximum(m_ref[...], mx_buf[slot])
        alpha_ref[...] = jnp.exp2(m_ref[...] - m_new)
        m_ref[...] = m_new
        p_buf[slot] = jnp.exp2(s_buf[slot] - jnp.concatenate([m_ref[...]] * (tk // LANES), axis=1)).astype(BF16)
        ones_col = jnp.where(lax.broadcasted_iota(jnp.int32, (tk, LANES), 1) == 0, 1.0, 0.0).astype(BF16)
        vb = jnp.concatenate([v_ref[0, pl.ds(pl.multiple_of(t * tk, tk), tk), :], ones_col], axis=1)
        pv = jnp.dot(p_buf[slot], vb, preferred_element_type=F32)
        acc_ref[...] = jnp.concatenate([alpha_ref[...]] * 2, axis=1) * acc_ref[...] + pv

    scores(0, 0)
    scores(1, 1)

    def body(i, carry):
        for u in range(2):
            softmax_pv(2 * i + u, u)
            scores(2 * i + u + 2, u)
        return carry

    lax.fori_loop(0, n_units // 2 - 1, body, 0)
    softmax_pv(n_units - 2, 0)
    softmax_pv(n_units - 1, 1)

    lam = (jnp.exp(jnp.sum(lq1_ref[...] * lk1_ref[...], axis=-1, keepdims=True))
           - jnp.exp(jnp.sum(lq2_ref[...] * lk2_ref[...], axis=-1, keepdims=True)) + lambda_init)
    a0, a1 = acc_ref[:tq], acc_ref[tq:]
    o = a0[:, :dv] / a0[:, dv:dv + 1] - lam * (a1[:, :dv] / a1[:, dv:dv + 1])
    o = o * lax.rsqrt(jnp.mean(o * o, axis=-1, keepdims=True) + EPS)
    o_ref[0] = (o * (subg_ref[...] * (1.0 - lambda_init))).astype(BF16)


def _diff_attention(qkv, lq1, lk1, lq2, lk2, subg, lambda_init):
    B, S, _ = qkv.shape
    H = DA_HEADS
    tq = min(DA_TQ, S)
    tk = min(2 * DA_TK if S // DA_TK <= DA_MIN_UNITS else DA_TK, S // 2)
    kern = functools.partial(_da_kernel, lambda_init=lambda_init, tk=tk)
    vec = _resident((1, DA_DK))
    return pl.pallas_call(
        kern,
        grid=(B, H, S // tq),
        in_specs=[
            pl.BlockSpec((1, tq, LANES), lambda b, h, i: (b, i, h)),
            pl.BlockSpec((1, S, LANES), lambda b, h, i: (b, 0, H + h), pipeline_mode=pl.Buffered(1)),
            pl.BlockSpec((1, S, LANES), lambda b, h, i: (b, 0, 2 * H + h), pipeline_mode=pl.Buffered(1)),
            vec, vec, vec, vec,
            _resident((1, LANES)),
        ],
        out_specs=pl.BlockSpec((1, tq, LANES), lambda b, h, i: (b, i, h)),
        out_shape=jax.ShapeDtypeStruct((B, S, H * LANES), BF16),
        scratch_shapes=[
            pltpu.VMEM((2 * tq, LANES), BF16),
            pltpu.VMEM((2, 2 * tq, tk), F32),
            pltpu.VMEM((2, 2 * tq, LANES), F32),
            pltpu.VMEM((2, 2 * tq, tk), BF16),
            pltpu.VMEM((2 * tq, LANES), F32),
            pltpu.VMEM((2 * tq, LANES), F32),
            pltpu.VMEM((2 * tq, MXU_DIM), F32),
        ],
        compiler_params=_cparams(("parallel", "parallel", "arbitrary")),
        name="diff_attn",
    )(qkv, qkv, qkv, lq1, lk1, lq2, lk2, subg)


def _wa_kernel(sink_ref, q_ref, kp_ref, kc_ref, kn_ref, vp_ref, vc_ref, vn_ref, o_ref, *, seq_len):
    i = pl.program_id(1)
    tq = q_ref.shape[1]
    nk = tq + 2 * WINDOW
    kcat = jnp.concatenate([kp_ref[0], kc_ref[0], kn_ref[0]], axis=0)
    vcat = jnp.concatenate([vp_ref[0], vc_ref[0], vn_ref[0]], axis=0)
    qpos = i * tq + lax.broadcasted_iota(jnp.int32, (tq, nk), 0)
    kpos = i * tq - WINDOW + lax.broadcasted_iota(jnp.int32, (tq, nk), 1)
    valid = (kpos >= 0) & (kpos < seq_len) & (jnp.abs(qpos - kpos) <= WINDOW)
    lane = lax.broadcasted_iota(jnp.int32, (tq, LANES), 1)
    lo = lane < WA_DH
    even = _in_group_a(lane)
    group = WA_HEADS // WA_KV_HEADS
    for pair in range(WA_HEADS // 2):
        qt = q_ref[0, :, pair * LANES:(pair + 1) * LANES]
        zero = jnp.zeros_like(qt)
        outs = []
        for half in range(2):
            h = 2 * pair + half
            g = h // group
            qh = jnp.where(even, qt, zero) if half == 0 else jnp.where(even, zero, qt)
            kg = kcat[:, g * LANES:(g + 1) * LANES]
            vg = vcat[:, g * LANES:(g + 1) * LANES]
            s = lax.dot_general(qh, kg, (((1,), (1,)), ((), ())), preferred_element_type=F32)
            s = jnp.where(valid, s, -jnp.inf)
            sink = sink_ref[h]
            m = jnp.maximum(jnp.max(s, axis=-1, keepdims=True), sink)
            p = jnp.exp2(s - m)
            denom = jnp.sum(p, axis=-1, keepdims=True) + jnp.exp2(sink - m)
            outs.append(jnp.dot(p.astype(BF16), vg, preferred_element_type=F32) / denom)
        o_ref[0, :, pair * LANES:(pair + 1) * LANES] = jnp.where(lo, outs[0], outs[1]).astype(BF16)


def _window_attention(qkv, sink):
    B, S, _ = qkv.shape
    tq = min(WA_TQ, S)
    qw = WA_HEADS * WA_DH
    kw = WA_KV_HEADS * LANES
    qb, kb, vb = 0, qw // kw, qw // kw + 1
    r = tq // WINDOW
    last = S // WINDOW - 1
    prev = lambda b, i: (b, jnp.maximum(i * r - 1, 0))
    nxt = lambda b, i: (b, jnp.minimum((i + 1) * r, last))
    kern = functools.partial(_wa_kernel, seq_len=S)
    return pl.pallas_call(
        kern,
        grid=(B, S // tq),
        in_specs=[
            pl.BlockSpec(memory_space=pltpu.SMEM),
            pl.BlockSpec((1, tq, qw), lambda b, i: (b, i, qb)),
            pl.BlockSpec((1, WINDOW, kw), lambda b, i: prev(b, i) + (kb,)),
            pl.BlockSpec((1, tq, kw), lambda b, i: (b, i, kb)),
            pl.BlockSpec((1, WINDOW, kw), lambda b, i: nxt(b, i) + (kb,)),
            pl.BlockSpec((1, WINDOW, kw), lambda b, i: prev(b, i) + (vb,)),
            pl.BlockSpec((1, tq, kw), lambda b, i: (b, i, vb)),
            pl.BlockSpec((1, WINDOW, kw), lambda b, i: nxt(b, i) + (vb,)),
        ],
        out_specs=pl.BlockSpec((1, tq, qw), lambda b, i: (b, i, 0)),
        out_shape=jax.ShapeDtypeStruct((B, S, qw), BF16),
        compiler_params=_cparams(("parallel", "parallel")),
        name="window_attn",
    )(sink, qkv, qkv, qkv, qkv, qkv, qkv, qkv)


def _fill_hext(hext, xp_ref, x_ref, xn_ref, gin):
    i = pl.program_id(1)
    tm = x_ref.shape[1]
    hp = _rmsnorm_bf16(xp_ref[0], gin)
    hn = _rmsnorm_bf16(xn_ref[0], gin)
    hext[0:HALO] = jnp.where(i > 0, hp, jnp.zeros_like(hp))
    hext[HALO:HALO + tm] = _rmsnorm_bf16(x_ref[0], gin)
    hext[HALO + tm:] = jnp.where(i < pl.num_programs(1) - 1, hn, jnp.zeros_like(hn))


def _halo_specs(S, tm, D):
    r = tm // HALO
    last = S // HALO - 1
    return [
        pl.BlockSpec((1, HALO, D), lambda b, i: (b, jnp.maximum(i * r - 1, 0), 0)),
        pl.BlockSpec((1, tm, D), lambda b, i: (b, i, 0)),
        pl.BlockSpec((1, HALO, D), lambda b, i: (b, jnp.minimum((i + 1) * r, last), 0)),
    ]


def _shift_rows(y, off, tm):
    if off == 0:
        return y[HALO:HALO + tm]
    return pltpu.roll(y, (-off) % y.shape[0], 0)[HALO:HALO + tm]


def _ffn_kernel(xp_ref, x_ref, xn_ref, gin_ref, wg_ref, wu_ref, cw_ref, wd_ref, o_ref, hext, g_buf, u_buf, acc):
    tm = x_ref.shape[1]
    nc = wg_ref.shape[0]
    _fill_hext(hext, xp_ref, x_ref, xn_ref, gin_ref[...])
    acc[...] = x_ref[0]

    def up(c):
        g_buf[c % 2] = jnp.dot(hext[...], wg_ref[c], preferred_element_type=F32)
        u_buf[c % 2] = jnp.dot(hext[HALO:HALO + tm], wu_ref[c], preferred_element_type=F32)

    def down(c):
        g = g_buf[c % 2]
        cw = cw_ref[c]
        gc = (_shift_rows(g, -1, tm) * cw[0:1] + _shift_rows(g, 0, tm) * cw[1:2]
              + _shift_rows(g, 1, tm) * cw[2:3] + cw[3:4])
        act = (gc * jax.nn.sigmoid(gc) * u_buf[c % 2]).astype(BF16)
        acc[...] += jnp.dot(act, wd_ref[c], preferred_element_type=F32)

    up(0)
    for c in range(nc):
        if c + 1 < nc:
            up(c + 1)
        down(c)
    o_ref[0] = acc[...]


def _conv_ffn(x, gin, wg, wu, cw, wd):
    B, S, D = x.shape
    nc, _, fc = wg.shape
    tm = min(FFN_TILE, S)
    return pl.pallas_call(
        _ffn_kernel,
        grid=(B, S // tm),
        in_specs=_halo_specs(S, tm, D) + [
            _resident((1, D)),
            _resident((nc, D, fc)),
            _resident((nc, D, fc)),
            _resident((nc, 8, fc)),
            _resident((nc, fc, D)),
        ],
        out_specs=pl.BlockSpec((1, tm, D), lambda b, i: (b, i, 0)),
        out_shape=jax.ShapeDtypeStruct((B, S, D), F32),
        scratch_shapes=[pltpu.VMEM((tm + 2 * HALO, D), BF16), pltpu.VMEM((2, tm + 2 * HALO, fc), F32),
                        pltpu.VMEM((2, tm, fc), F32), pltpu.VMEM((tm, D), F32)],
        compiler_params=_cparams(("parallel", "parallel")),
        name="conv_ffn",
    )(x, x, x, gin, wg, wu, cw, wd)


def _rgin_kernel(xp_ref, x_ref, xn_ref, gin_ref, wg_ref, wu_ref, cw_ref, gate_ref, u_ref, hext, *, cn):
    tm = x_ref.shape[1]
    _fill_hext(hext, xp_ref, x_ref, xn_ref, gin_ref[...])
    n_out = wg_ref.shape[1]
    for j in range(n_out // cn):
        cols = slice(j * cn, (j + 1) * cn)
        gate_ref[0, :, cols] = jnp.dot(hext[HALO:HALO + tm], wg_ref[:, cols], preferred_element_type=F32)
        u = jnp.dot(hext[...], wu_ref[:, cols], preferred_element_type=F32)
        cw = cw_ref[:, cols]
        u_ref[0, :, cols] = (_shift_rows(u, -2, tm) * cw[0:1] + _shift_rows(u, -1, tm) * cw[1:2]
                             + _shift_rows(u, 0, tm) * cw[2:3] + _shift_rows(u, 1, tm) * cw[3:4] + cw[4:5])


def _rg_in(x, gin, wg, wu, cw):
    B, S, D = x.shape
    C = wg.shape[1]
    tm = min(ROW_TILE, S)
    kern = functools.partial(_rgin_kernel, cn=512)
    out = jax.ShapeDtypeStruct((B, S, C), F32)
    ospec = pl.BlockSpec((1, tm, C), lambda b, i: (b, i, 0))
    return pl.pallas_call(
        kern,
        grid=(B, S // tm),
        in_specs=_halo_specs(S, tm, D) + [
            _resident((1, D)), _resident((D, C)), _resident((D, C)), _resident((8, C))],
        out_specs=[ospec, ospec],
        out_shape=[out, out],
        scratch_shapes=[pltpu.VMEM((tm + 2 * HALO, D), BF16)],
        compiler_params=_cparams(("parallel", "parallel")),
        name="rg_in",
    )(x, x, x, gin, wg, wu, cw)


def _rg_gate_tiles(C, bw):
    tiles = []
    for j in range(C // MXU_DIM):
        b_lo = (j * MXU_DIM) // bw
        b_hi = (j * MXU_DIM + MXU_DIM - 1) // bw
        k_lo = (b_lo * bw) // LANES * LANES
        k_hi = min(C, -(-((b_hi + 1) * bw) // LANES) * LANES)
        tiles.append((k_lo, k_hi))
    return tiles


def _rgscan_kernel(*refs, reverse, final, bw):
    if final:
        (u_ref, wr_ref, wi_ref, gb_ref, hf_ref, gate_ref, x_ref, wo_ref, o_ref,
         a_s, b_s, h_s, carry) = refs
    else:
        u_ref, wr_ref, wi_ref, gb_ref, o_ref, a_s, b_s, h_s, carry = refs
    tc, C = u_ref.shape[1], u_ref.shape[2]

    @pl.when(pl.program_id(1) == 0)
    def _init():
        carry[...] = jnp.zeros(carry.shape, F32)

    u = u_ref[0]
    ub = u.astype(BF16)
    for j, (k_lo, k_hi) in enumerate(_rg_gate_tiles(C, bw)):
        cols = slice(j * MXU_DIM, (j + 1) * MXU_DIM)
        rp = jnp.dot(ub[:, k_lo:k_hi], wr_ref[k_lo:k_hi, cols], preferred_element_type=F32) + gb_ref[0:1, cols]
        ip = jnp.dot(ub[:, k_lo:k_hi], wi_ref[k_lo:k_hi, cols], preferred_element_type=F32) + gb_ref[1:2, cols]
        log_a = (-RG_C * jax.nn.softplus(-gb_ref[2:3, cols])) * _sigmoid(rp)
        a = jnp.exp(log_a)
        a_s[:, cols] = a
        b_s[:, cols] = jnp.sqrt(-jnp.tanh(log_a) * (a * a + 1.0)) * (_sigmoid(ip) * u[:, cols])

    ng = tc // 8

    def body(gi, h):
        g = ng - 1 - gi if reverse else gi
        r0 = pl.multiple_of(g * 8, 8)
        a8 = a_s[pl.ds(r0, 8), :]
        b8 = b_s[pl.ds(r0, 8), :]
        rows = [None] * 8
        for t in (range(7, -1, -1) if reverse else range(8)):
            h = a8[t:t + 1] * h + b8[t:t + 1]
            rows[t] = h
        h_s[pl.ds(r0, 8), :] = jnp.concatenate(rows, axis=0)
        return h

    carry[0:1, :] = lax.fori_loop(0, ng, body, carry[0:1, :])
    if final:
        y = ((hf_ref[0] + h_s[...]) * jax.nn.gelu(gate_ref[0])).astype(BF16)
        o_ref[0] = x_ref[0] + jnp.dot(y, wo_ref[...], preferred_element_type=F32)
    else:
        o_ref[0] = h_s[...]


def _rg_scan(u, wr, wi, gb, *, reverse, extra=None):
    B, S, C = u.shape
    tc = min(RG_TC, S)
    n = S // tc
    idx = (lambda b, i: (b, n - 1 - i, 0)) if reverse else (lambda b, i: (b, i, 0))
    final = extra is not None
    in_specs = [pl.BlockSpec((1, tc, C), idx), _resident((C, C)), _resident((C, C)), _resident((8, C))]
    args = [u, wr, wi, gb]
    out_w = C
    if final:
        hf, gate, x, wo = extra
        out_w = x.shape[2]
        in_specs += [pl.BlockSpec((1, tc, C), idx), pl.BlockSpec((1, tc, C), idx),
                     pl.BlockSpec((1, tc, out_w), idx), _resident(wo.shape)]
        args += [hf, gate, x, wo]
    kern = functools.partial(_rgscan_kernel, reverse=reverse, final=final, bw=C // RG_BLOCKS)
    return pl.pallas_call(
        kern,
        grid=(B, n),
        in_specs=in_specs,
        out_specs=pl.BlockSpec((1, tc, out_w), idx),
        out_shape=jax.ShapeDtypeStruct((B, S, out_w), F32),
        scratch_shapes=[pltpu.VMEM((tc, C), F32), pltpu.VMEM((tc, C), F32), pltpu.VMEM((tc, C), F32),
                        pltpu.VMEM((8, C), F32)],
        compiler_params=_cparams(("parallel", "arbitrary")),
        name="rg_scan_final" if final else "rg_scan",
    )(*args)


def _rope_tables(S, dh):
    rot = dh // ROT_FRAC
    half = rot // 2
    inv = ROPE_THETA ** (-jnp.arange(half, dtype=F32) * 2.0 / rot)
    ang = jnp.arange(S, dtype=F32)[:, None] * inv[None, :]
    cos, sin = jnp.cos(ang), jnp.sin(ang)
    rest = dh - rot
    c = jnp.concatenate([cos, cos, jnp.ones((S, rest), F32)], axis=1)
    s = jnp.concatenate([-sin, sin, jnp.zeros((S, rest), F32)], axis=1)
    reps = LANES // dh
    return _rope_layout(jnp.tile(c, (1, reps)), 1), _rope_layout(jnp.tile(s, (1, reps)), 1)


def _block_diag(w):
    n, c, e = w.shape
    eye = jnp.eye(n, dtype=w.dtype)
    return (w[:, :, None, :] * eye[:, None, :, None]).reshape(n * c, n * e)


def _prepare(p):
    depth, D = p['norm_mix'].shape
    F = p['ffn_w_down'].shape[1]
    fc = MXU_DIM
    nc = F // fc
    prep = {'norm_mix': p['norm_mix'].reshape(depth, 1, D), 'norm_ffn': p['norm_ffn'].reshape(depth, 1, D)}
    wup = p['ffn_w_up'].astype(BF16)
    prep['ffn_wg'] = wup[:, :, :F].reshape(depth, D, nc, fc).transpose(0, 2, 1, 3)
    prep['ffn_wu'] = wup[:, :, F:].reshape(depth, D, nc, fc).transpose(0, 2, 1, 3)
    cw = jnp.concatenate([p['ffn_conv_w'], p['ffn_conv_b'][:, None, :]], axis=1)
    cw = jnp.pad(cw, ((0, 0), (0, 8 - cw.shape[1]), (0, 0)))
    prep['ffn_cw'] = cw.reshape(depth, 8, nc, fc).transpose(0, 2, 1, 3)
    prep['ffn_wd'] = p['ffn_w_down'].astype(BF16).reshape(depth, nc, fc, D)

    n_da_q = DA_HEADS * 2
    prep['da_w_qkv'] = _rope_layout(p['da_w_qkv'], n_da_q).astype(BF16)
    da_geff = jnp.concatenate([jnp.tile(p['da_q_norm'] * (DA_DK ** -0.5 * LOG2E), (1, n_da_q)),
                               jnp.tile(p['da_k_norm'], (1, n_da_q))], axis=1)
    prep['da_geff'] = _rope_layout(da_geff, n_da_q)[:, None, :]
    prep['da_w_o'] = p['da_w_o'].astype(BF16)
    for name in ('da_lambda_q1', 'da_lambda_k1', 'da_lambda_q2', 'da_lambda_k2', 'da_sub_norm'):
        prep[name] = p[name][:, None, :]

    qw, kvw = WA_HEADS * WA_DH, WA_KV_HEADS * WA_DH
    w = p['wa_w_qkv']
    n_b = w.shape[0]

    def dup(cols):
        c = cols.reshape(n_b, D, WA_KV_HEADS, 1, WA_DH)
        return jnp.broadcast_to(c, (n_b, D, WA_KV_HEADS, 2, WA_DH)).reshape(n_b, D, 2 * kvw)

    n_wa_qk = (qw + 2 * kvw) // LANES
    wa_w = jnp.concatenate([w[:, :, :qw], dup(w[:, :, qw:qw + kvw]), dup(w[:, :, qw + kvw:])], axis=2)
    prep['wa_w_qkv'] = _rope_layout(wa_w, n_wa_qk).astype(BF16)
    wa_geff = jnp.concatenate([jnp.tile(p['wa_q_norm'] * (WA_DH ** -0.5 * LOG2E), (1, WA_HEADS)),
                               jnp.tile(p['wa_k_norm'], (1, 2 * WA_KV_HEADS))], axis=1)
    prep['wa_geff'] = _rope_layout(wa_geff, n_wa_qk)[:, None, :]
    prep['wa_sink'] = p['wa_sink'] * LOG2E
    prep['wa_w_o'] = p['wa_w_o'].astype(BF16)

    C = p['rg_w_out'].shape[1]
    bw = C // RG_BLOCKS
    win = p['rg_w_in'].astype(BF16)
    prep['rg_wg'], prep['rg_wu'] = win[:, :, :C], win[:, :, C:]
    rcw = jnp.concatenate([p['rg_conv_w'], p['rg_conv_b'][:, None, :]], axis=1)
    prep['rg_cw'] = jnp.pad(rcw, ((0, 0), (0, 8 - rcw.shape[1]), (0, 0)))
    gw, gb = p['rg_gate_w'], p['rg_gate_b']
    n_c = gw.shape[0]
    prep['rg_wr'] = jnp.stack([jnp.stack([_block_diag(gw[j, d, :, :, :bw]) for d in range(2)])
                               for j in range(n_c)]).astype(BF16)
    prep['rg_wi'] = jnp.stack([jnp.stack([_block_diag(gw[j, d, :, :, bw:]) for d in range(2)])
                               for j in range(n_c)]).astype(BF16)
    gvec = jnp.stack([gb[..., :bw].reshape(n_c, 2, C), gb[..., bw:].reshape(n_c, 2, C), p['rg_lambda']], axis=2)
    prep['rg_gb'] = jnp.pad(gvec, ((0, 0), (0, 0), (0, 5), (0, 0)))
    prep['rg_w_out'] = p['rg_w_out'].astype(BF16)
    return prep


def _lambda_init(layer_idx):
    return 0.8 - 0.6 * math.exp(-0.3 * layer_idx)


def _trunk(x, w):
    S = x.shape[1]
    cos, sin = _rope_tables(S, DA_DK)
    depth = w['norm_mix'].shape[0]
    for i in range(depth):
        kind, j = i % N_MIXERS, i // N_MIXERS
        gin = w['norm_mix'][i]
        if kind == 0:
            qkv = _project(x, gin, w['da_w_qkv'][j], w['da_geff'][j], cos, sin, 2 * DA_HEADS * LANES)
            o = _diff_attention(qkv, w['da_lambda_q1'][j], w['da_lambda_k1'][j], w['da_lambda_q2'][j],
                                w['da_lambda_k2'][j], w['da_sub_norm'][j], _lambda_init(i))
            x = _outproj(o, w['da_w_o'][j], x)
        elif kind == 1:
            n_prep = WA_HEADS * WA_DH + WA_KV_HEADS * LANES
            qkv = _project(x, gin, w['wa_w_qkv'][j], w['wa_geff'][j], cos, sin, n_prep)
            o = _window_attention(qkv, w['wa_sink'][j])
            x = _outproj(o, w['wa_w_o'][j], x)
        else:
            gate, u = _rg_in(x, gin, w['rg_wg'][j], w['rg_wu'][j], w['rg_cw'][j])
            hf = _rg_scan(u, w['rg_wr'][j, 0], w['rg_wi'][j, 0], w['rg_gb'][j, 0], reverse=False)
            x = _rg_scan(u, w['rg_wr'][j, 1], w['rg_wi'][j, 1], w['rg_gb'][j, 1], reverse=True,
                         extra=(hf, gate, x, w['rg_w_out'][j]))
        x = _conv_ffn(x, w['norm_ffn'][i], w['ffn_wg'][i], w['ffn_wu'][i], w['ffn_cw'][i], w['ffn_wd'][i])
    return x


def kernel(x_prompt, x_sample, norm_mix, norm_ffn, ffn_w_up, ffn_conv_w, ffn_conv_b, ffn_w_down, da_w_qkv, da_q_norm, da_k_norm, da_lambda_q1, da_lambda_k1, da_lambda_q2, da_lambda_k2, da_sub_norm, da_w_o, wa_w_qkv, wa_q_norm, wa_k_norm, wa_sink, wa_w_o, rg_w_in, rg_conv_w, rg_conv_b, rg_gate_w, rg_gate_b, rg_lambda, rg_w_out):
    w = _prepare(dict(
        norm_mix=norm_mix, norm_ffn=norm_ffn, ffn_w_up=ffn_w_up, ffn_conv_w=ffn_conv_w, ffn_conv_b=ffn_conv_b,
        ffn_w_down=ffn_w_down, da_w_qkv=da_w_qkv, da_q_norm=da_q_norm, da_k_norm=da_k_norm,
        da_lambda_q1=da_lambda_q1, da_lambda_k1=da_lambda_k1, da_lambda_q2=da_lambda_q2,
        da_lambda_k2=da_lambda_k2, da_sub_norm=da_sub_norm, da_w_o=da_w_o, wa_w_qkv=wa_w_qkv,
        wa_q_norm=wa_q_norm, wa_k_norm=wa_k_norm, wa_sink=wa_sink, wa_w_o=wa_w_o, rg_w_in=rg_w_in,
        rg_conv_w=rg_conv_w, rg_conv_b=rg_conv_b, rg_gate_w=rg_gate_w, rg_gate_b=rg_gate_b,
        rg_lambda=rg_lambda, rg_w_out=rg_w_out))
    return (_trunk(x_prompt, w), _trunk(x_sample, w))
```

```python
import functools
import math

import jax
import jax.numpy as jnp
from jax import lax
from jax.experimental import pallas as pl
from jax.experimental.pallas import tpu as pltpu

F32 = jnp.float32
BF16 = jnp.bfloat16

DA_HEADS = 8
DA_DK = 64
WA_HEADS = 16
WA_KV_HEADS = 4
WA_DH = 64
WINDOW = 128
RG_BLOCKS = 16
RG_C = 8.0
ROPE_THETA = 500000.0
ROT_FRAC = 4
EPS = 1e-6
N_MIXERS = 3
LOG2E = math.log2(math.e)

LANES = 128
MXU_DIM = 256
HALO = 16
ROW_TILE = 512
FFN_TILE = 1024
DA_TQ, DA_TK = 1024, 512
DA_MIN_UNITS = 8
WA_TQ = 256
RG_TC = 256
VMEM_LIMIT = 56 * 1024 * 1024


def _cparams(sem):
    return pltpu.CompilerParams(dimension_semantics=sem, vmem_limit_bytes=VMEM_LIMIT)


def _rmsnorm_bf16(x, g):
    return (x * lax.rsqrt(jnp.mean(x * x, axis=-1, keepdims=True) + EPS) * g).astype(BF16)


def _sigmoid(x):
    return 0.5 * jnp.tanh(0.5 * x) + 0.5


def _in_group_a(lane):
    return (lane < 8) | ((lane >= 16) & (lane < 72))


def _rope_layout(a, n_tiles):
    lead = a.shape[:-1]
    head = a[..., :n_tiles * LANES].reshape(lead + (n_tiles, LANES))
    head = jnp.concatenate([head[..., 0:8], head[..., 64:72], head[..., 16:64],
                            head[..., 8:16], head[..., 72:LANES]], axis=-1)
    return jnp.concatenate([head.reshape(lead + (n_tiles * LANES,)), a[..., n_tiles * LANES:]], axis=-1)


def _resident(shape):
    nd = len(shape)
    return pl.BlockSpec(shape, lambda *_: (0,) * nd, pipeline_mode=pl.Buffered(1))


def _proj_kernel(x_ref, gin_ref, w_ref, geff_ref, cos_ref, sin_ref, o_ref, *, n_prep, cn):
    tm = x_ref.shape[1]
    n_out = w_ref.shape[1]
    h = _rmsnorm_bf16(x_ref[0], gin_ref[...])
    lo = _in_group_a(lax.broadcasted_iota(jnp.int32, (tm, LANES), 1))
    cos = cos_ref[...]
    sin = sin_ref[...]
    for j in range(n_out // cn):
        y = jnp.dot(h, w_ref[:, j * cn:(j + 1) * cn], preferred_element_type=F32)
        for t in range(cn // LANES):
            col = j * cn + t * LANES
            yt = y[:, t * LANES:(t + 1) * LANES]
            if col < n_prep:
                sq = yt * yt
                s_lo = jnp.sum(jnp.where(lo, sq, 0.0), axis=-1, keepdims=True)
                s_hi = jnp.sum(jnp.where(lo, 0.0, sq), axis=-1, keepdims=True)
                r = lax.rsqrt(jnp.where(lo, s_lo, s_hi) * (1.0 / 64.0) + EPS)
                yn = yt * r * geff_ref[:, col:col + LANES]
                yt = yn * cos + pltpu.roll(yn, LANES // 2, 1) * sin
            o_ref[0, :, col:col + LANES] = yt.astype(BF16)


def _project(x, gin, w, geff, cos, sin, n_prep):
    B, S, D = x.shape
    n_out = w.shape[1]
    tm = min(ROW_TILE, S)
    kern = functools.partial(_proj_kernel, n_prep=n_prep, cn=512)
    return pl.pallas_call(
        kern,
        grid=(B, S // tm),
        in_specs=[
            pl.BlockSpec((1, tm, D), lambda b, i: (b, i, 0)),
            _resident((1, D)),
            _resident((D, n_out)),
            _resident((1, n_prep)),
            pl.BlockSpec((tm, LANES), lambda b, i: (i, 0)),
            pl.BlockSpec((tm, LANES), lambda b, i: (i, 0)),
        ],
        out_specs=pl.BlockSpec((1, tm, n_out), lambda b, i: (b, i, 0)),
        out_shape=jax.ShapeDtypeStruct((B, S, n_out), BF16),
        compiler_params=_cparams(("parallel", "parallel")),
        name="proj",
    )(x, gin, w, geff, cos, sin)


def _outproj_kernel(a_ref, w_ref, x_ref, o_ref):
    o_ref[0] = x_ref[0] + jnp.dot(a_ref[0], w_ref[...], preferred_element_type=F32)


def _outproj(a, w, x):
    B, S, K = a.shape
    D = w.shape[1]
    tm = min(ROW_TILE, S)
    return pl.pallas_call(
        _outproj_kernel,
        grid=(B, S // tm),
        in_specs=[
            pl.BlockSpec((1, tm, K), lambda b, i: (b, i, 0)),
            _resident((K, D)),
            pl.BlockSpec((1, tm, D), lambda b, i: (b, i, 0)),
        ],
        out_specs=pl.BlockSpec((1, tm, D), lambda b, i: (b, i, 0)),
        out_shape=jax.ShapeDtypeStruct((B, S, D), F32),
        compiler_params=_cparams(("parallel", "parallel")),
        name="outproj",
    )(a, w, x)


def _da_kernel(q_ref, k_ref, v_ref, lq1_ref, lk1_ref, lq2_ref, lk2_ref, subg_ref, o_ref,
               qs_ref, s_buf, mx_buf, p_buf, m_ref, alpha_ref, acc_ref, *, lambda_init, tk):
    tq = q_ref.shape[1]
    n_units = k_ref.shape[1] // tk
    dv = LANES

    q = q_ref[0]
    map0 = _in_group_a(lax.broadcasted_iota(jnp.int32, q.shape, 1))
    zero = jnp.zeros_like(q)
    qs_ref[:tq] = jnp.where(map0, q, zero)
    qs_ref[tq:] = jnp.where(map0, zero, q)
    m_ref[...] = jnp.full(m_ref.shape, -jnp.inf, F32)
    acc_ref[...] = jnp.zeros(acc_ref.shape, F32)

    def scores(t, slot):
        kb = k_ref[0, pl.ds(pl.multiple_of(t * tk, tk), tk), :]
        s_buf[slot] = lax.dot_general(qs_ref[...], kb, (((1,), (1,)), ((), ())),
                                      preferred_element_type=F32)
        mx = s_buf[slot, :, 0:LANES]
        for c in range(1, tk // LANES):
            mx = jnp.maximum(mx, s_buf[slot, :, c * LANES:(c + 1) * LANES])
        mx_buf[slot] = jnp.broadcast_to(jnp.max(mx, axis=-1, keepdims=True), mx_buf.shape[1:])

    def softmax_pv(t, slot):
        m_new = jnp.maximum(m_ref[...], mx_buf[slot])
        alpha_ref[...] = jnp.exp2(m_ref[...] - m_new)
        m_ref[...] = m_new
        p_buf[slot] = jnp.exp2(s_buf[slot] - jnp.concatenate([m_ref[...]] * (tk // LANES), axis=1)).astype(BF16)
        ones_col = jnp.where(lax.broadcasted_iota(jnp.int32, (tk, LANES), 1) == 0, 1.0, 0.0).astype(BF16)
        vb = jnp.concatenate([v_ref[0, pl.ds(pl.multiple_of(t * tk, tk), tk), :], ones_col], axis=1)
        pv = jnp.dot(p_buf[slot], vb, preferred_element_type=F32)
        acc_ref[...] = jnp.concatenate([alpha_ref[...]] * 2, axis=1) * acc_ref[...] + pv

    scores(0, 0)
    scores(1, 1)

    def body(i, carry):
        for u in range(2):
            softmax_pv(2 * i + u, u)
            scores(2 * i + u + 2, u)
        return carry

    lax.fori_loop(0, n_units // 2 - 1, body, 0)
    softmax_pv(n_units - 2, 0)
    softmax_pv(n_units - 1, 1)

    lam = (jnp.exp(jnp.sum(lq1_ref[...] * lk1_ref[...], axis=-1, keepdims=True))
           - jnp.exp(jnp.sum(lq2_ref[...] * lk2_ref[...], axis=-1, keepdims=True)) + lambda_init)
    a0, a1 = acc_ref[:tq], acc_ref[tq:]
    o = a0[:, :dv] / a0[:, dv:dv + 1] - lam * (a1[:, :dv] / a1[:, dv:dv + 1])
    o = o * lax.rsqrt(jnp.mean(o * o, axis=-1, keepdims=True) + EPS)
    o_ref[0] = (o * (subg_ref[...] * (1.0 - lambda_init))).astype(BF16)


def _diff_attention(qkv, lq1, lk1, lq2, lk2, subg, lambda_init):
    B, S, _ = qkv.shape
    H = DA_HEADS
    tq = min(DA_TQ, S)
    tk = min(2 * DA_TK if S // DA_TK <= DA_MIN_UNITS else DA_TK, S // 2)
    kern = functools.partial(_da_kernel, lambda_init=lambda_init, tk=tk)
    vec = _resident((1, DA_DK))
    return pl.pallas_call(
        kern,
        grid=(B, H, S // tq),
        in_specs=[
            pl.BlockSpec((1, tq, LANES), lambda b, h, i: (b, i, h)),
            pl.BlockSpec((1, S, LANES), lambda b, h, i: (b, 0, H + h)),
            pl.BlockSpec((1, S, LANES), lambda b, h, i: (b, 0, 2 * H + h)),
            vec, vec, vec, vec,
            _resident((1, LANES)),
        ],
        out_specs=pl.BlockSpec((1, tq, LANES), lambda b, h, i: (b, i, h)),
        out_shape=jax.ShapeDtypeStruct((B, S, H * LANES), BF16),
        scratch_shapes=[
            pltpu.VMEM((2 * tq, LANES), BF16),
            pltpu.VMEM((2, 2 * tq, tk), F32),
            pltpu.VMEM((2, 2 * tq, LANES), F32),
            pltpu.VMEM((2, 2 * tq, tk), BF16),
            pltpu.VMEM((2 * tq, LANES), F32),
            pltpu.VMEM((2 * tq, LANES), F32),
            pltpu.VMEM((2 * tq, MXU_DIM), F32),
        ],
        compiler_params=_cparams(("parallel", "parallel", "arbitrary")),
        name="diff_attn",
    )(qkv, qkv, qkv, lq1, lk1, lq2, lk2, subg)


def _wa_kernel(sink_ref, q_ref, kp_ref, kc_ref, kn_ref, vp_ref, vc_ref, vn_ref, o_ref, *, seq_len):
    i = pl.program_id(1)
    tq = q_ref.shape[1]
    nk = tq + 2 * WINDOW
    kcat = jnp.concatenate([kp_ref[0], kc_ref[0], kn_ref[0]], axis=0)
    vcat = jnp.concatenate([vp_ref[0], vc_ref[0], vn_ref[0]], axis=0)
    qpos = i * tq + lax.broadcasted_iota(jnp.int32, (tq, nk), 0)
    kpos = i * tq - WINDOW + lax.broadcasted_iota(jnp.int32, (tq, nk), 1)
    valid = (kpos >= 0) & (kpos < seq_len) & (jnp.abs(qpos - kpos) <= WINDOW)
    lane = lax.broadcasted_iota(jnp.int32, (tq, LANES), 1)
    lo = lane < WA_DH
    even = _in_group_a(lane)
    group = WA_HEADS // WA_KV_HEADS
    for pair in range(WA_HEADS // 2):
        qt = q_ref[0, :, pair * LANES:(pair + 1) * LANES]
        zero = jnp.zeros_like(qt)
        outs = []
        for half in range(2):
            h = 2 * pair + half
            g = h // group
            qh = jnp.where(even, qt, zero) if half == 0 else jnp.where(even, zero, qt)
            kg = kcat[:, g * LANES:(g + 1) * LANES]
            vg = vcat[:, g * LANES:(g + 1) * LANES]
            s = lax.dot_general(qh, kg, (((1,), (1,)), ((), ())), preferred_element_type=F32)
            s = jnp.where(valid, s, -jnp.inf)
            sink = sink_ref[h]
            m = jnp.maximum(jnp.max(s, axis=-1, keepdims=True), sink)
            p = jnp.exp2(s - m)
            denom = jnp.sum(p, axis=-1, keepdims=True) + jnp.exp2(sink - m)
            outs.append(jnp.dot(p.astype(BF16), vg, preferred_element_type=F32) / denom)
        o_ref[0, :, pair * LANES:(pair + 1) * LANES] = jnp.where(lo, outs[0], outs[1]).astype(BF16)


def _window_attention(qkv, sink):
    B, S, _ = qkv.shape
    tq = min(WA_TQ, S)
    qw = WA_HEADS * WA_DH
    kw = WA_KV_HEADS * LANES
    qb, kb, vb = 0, qw // kw, qw // kw + 1
    r = tq // WINDOW
    last = S // WINDOW - 1
    prev = lambda b, i: (b, jnp.maximum(i * r - 1, 0))
    nxt = lambda b, i: (b, jnp.minimum((i + 1) * r, last))
    kern = functools.partial(_wa_kernel, seq_len=S)
    return pl.pallas_call(
        kern,
        grid=(B, S // tq),
        in_specs=[
            pl.BlockSpec(memory_space=pltpu.SMEM),
            pl.BlockSpec((1, tq, qw), lambda b, i: (b, i, qb)),
            pl.BlockSpec((1, WINDOW, kw), lambda b, i: prev(b, i) + (kb,)),
            pl.BlockSpec((1, tq, kw), lambda b, i: (b, i, kb)),
            pl.BlockSpec((1, WINDOW, kw), lambda b, i: nxt(b, i) + (kb,)),
            pl.BlockSpec((1, WINDOW, kw), lambda b, i: prev(b, i) + (vb,)),
            pl.BlockSpec((1, tq, kw), lambda b, i: (b, i, vb)),
            pl.BlockSpec((1, WINDOW, kw), lambda b, i: nxt(b, i) + (vb,)),
        ],
        out_specs=pl.BlockSpec((1, tq, qw), lambda b, i: (b, i, 0)),
        out_shape=jax.ShapeDtypeStruct((B, S, qw), BF16),
        compiler_params=_cparams(("parallel", "parallel")),
        name="window_attn",
    )(sink, qkv, qkv, qkv, qkv, qkv, qkv, qkv)


def _fill_hext(hext, xp_ref, x_ref, xn_ref, gin):
    i = pl.program_id(1)
    tm = x_ref.shape[1]
    hp = _rmsnorm_bf16(xp_ref[0], gin)
    hn = _rmsnorm_bf16(xn_ref[0], gin)
    hext[0:HALO] = jnp.where(i > 0, hp, jnp.zeros_like(hp))
    hext[HALO:HALO + tm] = _rmsnorm_bf16(x_ref[0], gin)
    hext[HALO + tm:] = jnp.where(i < pl.num_programs(1) - 1, hn, jnp.zeros_like(hn))


def _halo_specs(S, tm, D):
    r = tm // HALO
    last = S // HALO - 1
    return [
        pl.BlockSpec((1, HALO, D), lambda b, i: (b, jnp.maximum(i * r - 1, 0), 0)),
        pl.BlockSpec((1, tm, D), lambda b, i: (b, i, 0)),
        pl.BlockSpec((1, HALO, D), lambda b, i: (b, jnp.minimum((i + 1) * r, last), 0)),
    ]


def _shift_rows(y, off, tm):
    if off == 0:
        return y[HALO:HALO + tm]
    return pltpu.roll(y, (-off) % y.shape[0], 0)[HALO:HALO + tm]


def _ffn_kernel(xp_ref, x_ref, xn_ref, gin_ref, wg_ref, wu_ref, cw_ref, wd_ref, o_ref, hext, g_buf, u_buf, acc):
    tm = x_ref.shape[1]
    nc = wg_ref.shape[0]
    _fill_hext(hext, xp_ref, x_ref, xn_ref, gin_ref[...])
    acc[...] = x_ref[0]

    def up(c):
        g_buf[c % 2] = jnp.dot(hext[...], wg_ref[c], preferred_element_type=F32)
        u_buf[c % 2] = jnp.dot(hext[HALO:HALO + tm], wu_ref[c], preferred_element_type=F32)

    def down(c):
        g = g_buf[c % 2]
        cw = cw_ref[c]
        gc = (_shift_rows(g, -1, tm) * cw[0:1] + _shift_rows(g, 0, tm) * cw[1:2]
              + _shift_rows(g, 1, tm) * cw[2:3] + cw[3:4])
        act = (gc * jax.nn.sigmoid(gc) * u_buf[c % 2]).astype(BF16)
        acc[...] += jnp.dot(act, wd_ref[c], preferred_element_type=F32)

    up(0)
    for c in range(nc):
        if c + 1 < nc:
            up(c + 1)
        down(c)
    o_ref[0] = acc[...]


def _conv_ffn(x, gin, wg, wu, cw, wd):
    B, S, D = x.shape
    nc, _, fc = wg.shape
    tm = min(FFN_TILE, S)
    return pl.pallas_call(
        _ffn_kernel,
        grid=(B, S // tm),
        in_specs=_halo_specs(S, tm, D) + [
            _resident((1, D)),
            _resident((nc, D, fc)),
            _resident((nc, D, fc)),
            _resident((nc, 8, fc)),
            _resident((nc, fc, D)),
        ],
        out_specs=pl.BlockSpec((1, tm, D), lambda b, i: (b, i, 0)),
        out_shape=jax.ShapeDtypeStruct((B, S, D), F32),
        scratch_shapes=[pltpu.VMEM((tm + 2 * HALO, D), BF16), pltpu.VMEM((2, tm + 2 * HALO, fc), F32),
                        pltpu.VMEM((2, tm, fc), F32), pltpu.VMEM((tm, D), F32)],
        compiler_params=_cparams(("parallel", "parallel")),
        name="conv_ffn",
    )(x, x, x, gin, wg, wu, cw, wd)


def _rgin_kernel(xp_ref, x_ref, xn_ref, gin_ref, wg_ref, wu_ref, cw_ref, gate_ref, u_ref, hext, *, cn):
    tm = x_ref.shape[1]
    _fill_hext(hext, xp_ref, x_ref, xn_ref, gin_ref[...])
    n_out = wg_ref.shape[1]
    for j in range(n_out // cn):
        cols = slice(j * cn, (j + 1) * cn)
        gate_ref[0, :, cols] = jnp.dot(hext[HALO:HALO + tm], wg_ref[:, cols], preferred_element_type=F32)
        u = jnp.dot(hext[...], wu_ref[:, cols], preferred_element_type=F32)
        cw = cw_ref[:, cols]
        u_ref[0, :, cols] = (_shift_rows(u, -2, tm) * cw[0:1] + _shift_rows(u, -1, tm) * cw[1:2]
                             + _shift_rows(u, 0, tm) * cw[2:3] + _shift_rows(u, 1, tm) * cw[3:4] + cw[4:5])


def _rg_in(x, gin, wg, wu, cw):
    B, S, D = x.shape
    C = wg.shape[1]
    tm = min(ROW_TILE, S)
    kern = functools.partial(_rgin_kernel, cn=512)
    out = jax.ShapeDtypeStruct((B, S, C), F32)
    ospec = pl.BlockSpec((1, tm, C), lambda b, i: (b, i, 0))
    return pl.pallas_call(
        kern,
        grid=(B, S // tm),
        in_specs=_halo_specs(S, tm, D) + [
            _resident((1, D)), _resident((D, C)), _resident((D, C)), _resident((8, C))],
        out_specs=[ospec, ospec],
        out_shape=[out, out],
        scratch_shapes=[pltpu.VMEM((tm + 2 * HALO, D), BF16)],
        compiler_params=_cparams(("parallel", "parallel")),
        name="rg_in",
    )(x, x, x, gin, wg, wu, cw)


def _rg_gate_tiles(C, bw):
    tiles = []
    for j in range(C // MXU_DIM):
        b_lo = (j * MXU_DIM) // bw
        b_hi = (j * MXU_DIM + MXU_DIM - 1) // bw
        k_lo = (b_lo * bw) // LANES * LANES
        k_hi = min(C, -(-((b_hi + 1) * bw) // LANES) * LANES)
        tiles.append((k_lo, k_hi))
    return tiles


def _rgscan_kernel(*refs, reverse, final, bw):
    if final:
        (u_ref, wr_ref, wi_ref, gb_ref, hf_ref, gate_ref, x_ref, wo_ref, o_ref,
         a_s, b_s, h_s, carry) = refs
    else:
        u_ref, wr_ref, wi_ref, gb_ref, o_ref, a_s, b_s, h_s, carry = refs
    tc, C = u_ref.shape[1], u_ref.shape[2]

    @pl.when(pl.program_id(1) == 0)
    def _init():
        carry[...] = jnp.zeros(carry.shape, F32)

    u = u_ref[0]
    ub = u.astype(BF16)
    for j, (k_lo, k_hi) in enumerate(_rg_gate_tiles(C, bw)):
        cols = slice(j * MXU_DIM, (j + 1) * MXU_DIM)
        rp = jnp.dot(ub[:, k_lo:k_hi], wr_ref[k_lo:k_hi, cols], preferred_element_type=F32) + gb_ref[0:1, cols]
        ip = jnp.dot(ub[:, k_lo:k_hi], wi_ref[k_lo:k_hi, cols], preferred_element_type=F32) + gb_ref[1:2, cols]
        log_a = (-RG_C * jax.nn.softplus(-gb_ref[2:3, cols])) * _sigmoid(rp)
        a = jnp.exp(log_a)
        a_s[:, cols] = a
        b_s[:, cols] = jnp.sqrt(-jnp.tanh(log_a) * (a * a + 1.0)) * (_sigmoid(ip) * u[:, cols])

    ng = tc // 8

    def body(gi, h):
        g = ng - 1 - gi if reverse else gi
        r0 = pl.multiple_of(g * 8, 8)
        a8 = a_s[pl.ds(r0, 8), :]
        b8 = b_s[pl.ds(r0, 8), :]
        rows = [None] * 8
        for t in (range(7, -1, -1) if reverse else range(8)):
            h = a8[t:t + 1] * h + b8[t:t + 1]
            rows[t] = h
        h_s[pl.ds(r0, 8), :] = jnp.concatenate(rows, axis=0)
        return h

    carry[0:1, :] = lax.fori_loop(0, ng, body, carry[0:1, :])
    if final:
        y = ((hf_ref[0] + h_s[...]) * jax.nn.gelu(gate_ref[0])).astype(BF16)
        o_ref[0] = x_ref[0] + jnp.dot(y, wo_ref[...], preferred_element_type=F32)
    else:
        o_ref[0] = h_s[...]


def _rg_scan(u, wr, wi, gb, *, reverse, extra=None):
    B, S, C = u.shape
    tc = min(RG_TC, S)
    n = S // tc
    idx = (lambda b, i: (b, n - 1 - i, 0)) if reverse else (lambda b, i: (b, i, 0))
    final = extra is not None
    in_specs = [pl.BlockSpec((1, tc, C), idx), _resident((C, C)), _resident((C, C)), _resident((8, C))]
    args = [u, wr, wi, gb]
    out_w = C
    if final:
        hf, gate, x, wo = extra
        out_w = x.shape[2]
        in_specs += [pl.BlockSpec((1, tc, C), idx), pl.BlockSpec((1, tc, C), idx),
                     pl.BlockSpec((1, tc, out_w), idx), _resident(wo.shape)]
        args += [hf, gate, x, wo]
    kern = functools.partial(_rgscan_kernel, reverse=reverse, final=final, bw=C // RG_BLOCKS)
    return pl.pallas_call(
        kern,
        grid=(B, n),
        in_specs=in_specs,
        out_specs=pl.BlockSpec((1, tc, out_w), idx),
        out_shape=jax.ShapeDtypeStruct((B, S, out_w), F32),
        scratch_shapes=[pltpu.VMEM((tc, C), F32), pltpu.VMEM((tc, C), F32), pltpu.VMEM((tc, C), F32),
                        pltpu.VMEM((8, C), F32)],
        compiler_params=_cparams(("parallel", "arbitrary")),
        name="rg_scan_final" if final else "rg_scan",
    )(*args)


def _rope_tables(S, dh):
    rot = dh // ROT_FRAC
    half = rot // 2
    inv = ROPE_THETA ** (-jnp.arange(half, dtype=F32) * 2.0 / rot)
    ang = jnp.arange(S, dtype=F32)[:, None] * inv[None, :]
    cos, sin = jnp.cos(ang), jnp.sin(ang)
    rest = dh - rot
    c = jnp.concatenate([cos, cos, jnp.ones((S, rest), F32)], axis=1)
    s = jnp.concatenate([-sin, sin, jnp.zeros((S, rest), F32)], axis=1)
    reps = LANES // dh
    return _rope_layout(jnp.tile(c, (1, reps)), 1), _rope_layout(jnp.tile(s, (1, reps)), 1)


def _block_diag(w):
    n, c, e = w.shape
    eye = jnp.eye(n, dtype=w.dtype)
    return (w[:, :, None, :] * eye[:, None, :, None]).reshape(n * c, n * e)


def _prepare(p):
    depth, D = p['norm_mix'].shape
    F = p['ffn_w_down'].shape[1]
    fc = MXU_DIM
    nc = F // fc
    prep = {'norm_mix': p['norm_mix'].reshape(depth, 1, D), 'norm_ffn': p['norm_ffn'].reshape(depth, 1, D)}
    wup = p['ffn_w_up'].astype(BF16)
    prep['ffn_wg'] = wup[:, :, :F].reshape(depth, D, nc, fc).transpose(0, 2, 1, 3)
    prep['ffn_wu'] = wup[:, :, F:].reshape(depth, D, nc, fc).transpose(0, 2, 1, 3)
    cw = jnp.concatenate([p['ffn_conv_w'], p['ffn_conv_b'][:, None, :]], axis=1)
    cw = jnp.pad(cw, ((0, 0), (0, 8 - cw.shape[1]), (0, 0)))
    prep['ffn_cw'] = cw.reshape(depth, 8, nc, fc).transpose(0, 2, 1, 3)
    prep['ffn_wd'] = p['ffn_w_down'].astype(BF16).reshape(depth, nc, fc, D)

    n_da_q = DA_HEADS * 2
    prep['da_w_qkv'] = _rope_layout(p['da_w_qkv'], n_da_q).astype(BF16)
    da_geff = jnp.concatenate([jnp.tile(p['da_q_norm'] * (DA_DK ** -0.5 * LOG2E), (1, n_da_q)),
                               jnp.tile(p['da_k_norm'], (1, n_da_q))], axis=1)
    prep['da_geff'] = _rope_layout(da_geff, n_da_q)[:, None, :]
    prep['da_w_o'] = p['da_w_o'].astype(BF16)
    for name in ('da_lambda_q1', 'da_lambda_k1', 'da_lambda_q2', 'da_lambda_k2', 'da_sub_norm'):
        prep[name] = p[name][:, None, :]

    qw, kvw = WA_HEADS * WA_DH, WA_KV_HEADS * WA_DH
    w = p['wa_w_qkv']
    n_b = w.shape[0]

    def dup(cols):
        c = cols.reshape(n_b, D, WA_KV_HEADS, 1, WA_DH)
        return jnp.broadcast_to(c, (n_b, D, WA_KV_HEADS, 2, WA_DH)).reshape(n_b, D, 2 * kvw)

    n_wa_qk = (qw + 2 * kvw) // LANES
    wa_w = jnp.concatenate([w[:, :, :qw], dup(w[:, :, qw:qw + kvw]), dup(w[:, :, qw + kvw:])], axis=2)
    prep['wa_w_qkv'] = _rope_layout(wa_w, n_wa_qk).astype(BF16)
    wa_geff = jnp.concatenate([jnp.tile(p['wa_q_norm'] * (WA_DH ** -0.5 * LOG2E), (1, WA_HEADS)),
                               jnp.tile(p['wa_k_norm'], (1, 2 * WA_KV_HEADS))], axis=1)
    prep['wa_geff'] = _rope_layout(wa_geff, n_wa_qk)[:, None, :]
    prep['wa_sink'] = p['wa_sink'] * LOG2E
    prep['wa_w_o'] = p['wa_w_o'].astype(BF16)

    C = p['rg_w_out'].shape[1]
    bw = C // RG_BLOCKS
    win = p['rg_w_in'].astype(BF16)
    prep['rg_wg'], prep['rg_wu'] = win[:, :, :C], win[:, :, C:]
    rcw = jnp.concatenate([p['rg_conv_w'], p['rg_conv_b'][:, None, :]], axis=1)
    prep['rg_cw'] = jnp.pad(rcw, ((0, 0), (0, 8 - rcw.shape[1]), (0, 0)))
    gw, gb = p['rg_gate_w'], p['rg_gate_b']
    n_c = gw.shape[0]
    prep['rg_wr'] = jnp.stack([jnp.stack([_block_diag(gw[j, d, :, :, :bw]) for d in range(2)])
                               for j in range(n_c)]).astype(BF16)
    prep['rg_wi'] = jnp.stack([jnp.stack([_block_diag(gw[j, d, :, :, bw:]) for d in range(2)])
                               for j in range(n_c)]).astype(BF16)
    gvec = jnp.stack([gb[..., :bw].reshape(n_c, 2, C), gb[..., bw:].reshape(n_c, 2, C), p['rg_lambda']], axis=2)
    prep['rg_gb'] = jnp.pad(gvec, ((0, 0), (0, 0), (0, 5), (0, 0)))
    prep['rg_w_out'] = p['rg_w_out'].astype(BF16)
    return prep


def _lambda_init(layer_idx):
    return 0.8 - 0.6 * math.exp(-0.3 * layer_idx)


def _trunk(x, w):
    S = x.shape[1]
    cos, sin = _rope_tables(S, DA_DK)
    depth = w['norm_mix'].shape[0]
    for i in range(depth):
        kind, j = i % N_MIXERS, i // N_MIXERS
        gin = w['norm_mix'][i]
        if kind == 0:
            qkv = _project(x, gin, w['da_w_qkv'][j], w['da_geff'][j], cos, sin, 2 * DA_HEADS * LANES)
            o = _diff_attention(qkv, w['da_lambda_q1'][j], w['da_lambda_k1'][j], w['da_lambda_q2'][j],
                                w['da_lambda_k2'][j], w['da_sub_norm'][j], _lambda_init(i))
            x = _outproj(o, w['da_w_o'][j], x)
        elif kind == 1:
            n_prep = WA_HEADS * WA_DH + WA_KV_HEADS * LANES
            qkv = _project(x, gin, w['wa_w_qkv'][j], w['wa_geff'][j], cos, sin, n_prep)
            o = _window_attention(qkv, w['wa_sink'][j])
            x = _outproj(o, w['wa_w_o'][j], x)
        else:
            gate, u = _rg_in(x, gin, w['rg_wg'][j], w['rg_wu'][j], w['rg_cw'][j])
            hf = _rg_scan(u, w['rg_wr'][j, 0], w['rg_wi'][j, 0], w['rg_gb'][j, 0], reverse=False)
            x = _rg_scan(u, w['rg_wr'][j, 1], w['rg_wi'][j, 1], w['rg_gb'][j, 1], reverse=True,
                         extra=(hf, gate, x, w['rg_w_out'][j]))
        x = _conv_ffn(x, w['norm_ffn'][i], w['ffn_wg'][i], w['ffn_wu'][i], w['ffn_cw'][i], w['ffn_wd'][i])
    return x


def kernel(x_prompt, x_sample, norm_mix, norm_ffn, ffn_w_up, ffn_conv_w, ffn_conv_b, ffn_w_down, da_w_qkv, da_q_norm, da_k_norm, da_lambda_q1, da_lambda_k1, da_lambda_q2, da_lambda_k2, da_sub_norm, da_w_o, wa_w_qkv, wa_q_norm, wa_k_norm, wa_sink, wa_w_o, rg_w_in, rg_conv_w, rg_conv_b, rg_gate_w, rg_gate_b, rg_lambda, rg_w_out):
    w = _prepare(dict(
        norm_mix=norm_mix, norm_ffn=norm_ffn, ffn_w_up=ffn_w_up, ffn_conv_w=ffn_conv_w, ffn_conv_b=ffn_conv_b,
        ffn_w_down=ffn_w_down, da_w_qkv=da_w_qkv, da_q_norm=da_q_norm, da_k_norm=da_k_norm,
        da_lambda_q1=da_lambda_q1, da_lambda_k1=da_lambda_k1, da_lambda_q2=da_lambda_q2,
        da_lambda_k2=da_lambda_k2, da_sub_norm=da_sub_norm, da_w_o=da_w_o, wa_w_qkv=wa_w_qkv,
        wa_q_norm=wa_q_norm, wa_k_norm=wa_k_norm, wa_sink=wa_sink, wa_w_o=wa_w_o, rg_w_in=rg_w_in,
        rg_conv_w=rg_conv_w, rg_conv_b=rg_conv_b, rg_gate_w=rg_gate_w, rg_gate_b=rg_gate_b,
        rg_lambda=rg_lambda, rg_w_out=rg_w_out))
    return (_trunk(x_prompt, w), _trunk(x_sample, w))
```

```python
import functools
import math

import jax
import jax.numpy as jnp
from jax import lax
from jax.experimental import pallas as pl
from jax.experimental.pallas import tpu as pltpu

F32 = jnp.float32
BF16 = jnp.bfloat16

DA_HEADS = 8
DA_DK = 64
WA_HEADS = 16
WA_KV_HEADS = 4
WA_DH = 64
WINDOW = 128
RG_BLOCKS = 16
RG_C = 8.0
ROPE_THETA = 500000.0
ROT_FRAC = 4
EPS = 1e-6
N_MIXERS = 3
LOG2E = math.log2(math.e)

LANES = 128
MXU_DIM = 256
HALO = 16
ROW_TILE = 512
FFN_TILE = 1024
DA_TQ, DA_TK = 512, 1024
WA_TQ = 256
RG_TC = 256
VMEM_LIMIT = 56 * 1024 * 1024


def _cparams(sem):
    return pltpu.CompilerParams(dimension_semantics=sem, vmem_limit_bytes=VMEM_LIMIT)


def _rmsnorm_bf16(x, g):
    return (x * lax.rsqrt(jnp.mean(x * x, axis=-1, keepdims=True) + EPS) * g).astype(BF16)


def _sigmoid(x):
    return 0.5 * jnp.tanh(0.5 * x) + 0.5


def _in_group_a(lane):
    return (lane < 8) | ((lane >= 16) & (lane < 72))


def _rope_layout(a, n_tiles):
    lead = a.shape[:-1]
    head = a[..., :n_tiles * LANES].reshape(lead + (n_tiles, LANES))
    head = jnp.concatenate([head[..., 0:8], head[..., 64:72], head[..., 16:64],
                            head[..., 8:16], head[..., 72:LANES]], axis=-1)
    return jnp.concatenate([head.reshape(lead + (n_tiles * LANES,)), a[..., n_tiles * LANES:]], axis=-1)


def _resident(shape):
    nd = len(shape)
    return pl.BlockSpec(shape, lambda *_: (0,) * nd, pipeline_mode=pl.Buffered(1))


def _proj_kernel(x_ref, gin_ref, w_ref, geff_ref, cos_ref, sin_ref, o_ref, *, n_prep, cn):
    tm = x_ref.shape[1]
    n_out = w_ref.shape[1]
    h = _rmsnorm_bf16(x_ref[0], gin_ref[...])
    lo = _in_group_a(lax.broadcasted_iota(jnp.int32, (tm, LANES), 1))
    cos = cos_ref[...]
    sin = sin_ref[...]
    for j in range(n_out // cn):
        y = jnp.dot(h, w_ref[:, j * cn:(j + 1) * cn], preferred_element_type=F32)
        for t in range(cn // LANES):
            col = j * cn + t * LANES
            yt = y[:, t * LANES:(t + 1) * LANES]
            if col < n_prep:
                sq = yt * yt
                s_lo = jnp.sum(jnp.where(lo, sq, 0.0), axis=-1, keepdims=True)
                s_hi = jnp.sum(jnp.where(lo, 0.0, sq), axis=-1, keepdims=True)
                r = lax.rsqrt(jnp.where(lo, s_lo, s_hi) * (1.0 / 64.0) + EPS)
                yn = yt * r * geff_ref[:, col:col + LANES]
                yt = yn * cos + pltpu.roll(yn, LANES // 2, 1) * sin
            o_ref[0, :, col:col + LANES] = yt.astype(BF16)


def _project(x, gin, w, geff, cos, sin, n_prep):
    B, S, D = x.shape
    n_out = w.shape[1]
    tm = min(ROW_TILE, S)
    kern = functools.partial(_proj_kernel, n_prep=n_prep, cn=512)
    return pl.pallas_call(
        kern,
        grid=(B, S // tm),
        in_specs=[
            pl.BlockSpec((1, tm, D), lambda b, i: (b, i, 0)),
            _resident((1, D)),
            _resident((D, n_out)),
            _resident((1, n_prep)),
            pl.BlockSpec((tm, LANES), lambda b, i: (i, 0)),
            pl.BlockSpec((tm, LANES), lambda b, i: (i, 0)),
        ],
        out_specs=pl.BlockSpec((1, tm, n_out), lambda b, i: (b, i, 0)),
        out_shape=jax.ShapeDtypeStruct((B, S, n_out), BF16),
        compiler_params=_cparams(("parallel", "parallel")),
        name="proj",
    )(x, gin, w, geff, cos, sin)


def _outproj_kernel(a_ref, w_ref, x_ref, o_ref):
    o_ref[0] = x_ref[0] + jnp.dot(a_ref[0], w_ref[...], preferred_element_type=F32)


def _outproj(a, w, x):
    B, S, K = a.shape
    D = w.shape[1]
    tm = min(ROW_TILE, S)
    return pl.pallas_call(
        _outproj_kernel,
        grid=(B, S // tm),
        in_specs=[
            pl.BlockSpec((1, tm, K), lambda b, i: (b, i, 0)),
            _resident((K, D)),
            pl.BlockSpec((1, tm, D), lambda b, i: (b, i, 0)),
        ],
        out_specs=pl.BlockSpec((1, tm, D), lambda b, i: (b, i, 0)),
        out_shape=jax.ShapeDtypeStruct((B, S, D), F32),
        compiler_params=_cparams(("parallel", "parallel")),
        name="outproj",
    )(a, w, x)


def _da_kernel(q_ref, k_ref, v_ref, lq1_ref, lk1_ref, lq2_ref, lk2_ref, subg_ref, o_ref,
               qs_ref, s_buf, mx_buf, p_buf, m_ref, alpha_ref, acc_ref, *, lambda_init, tk):
    tq = q_ref.shape[1]
    n_units = k_ref.shape[1] // tk
    dv = LANES

    q = q_ref[0]
    map0 = _in_group_a(lax.broadcasted_iota(jnp.int32, q.shape, 1))
    zero = jnp.zeros_like(q)
    qs_ref[:tq] = jnp.where(map0, q, zero)
    qs_ref[tq:] = jnp.where(map0, zero, q)
    m_ref[...] = jnp.full(m_ref.shape, -jnp.inf, F32)
    acc_ref[...] = jnp.zeros(acc_ref.shape, F32)

    def scores(t, slot):
        kb = k_ref[0, pl.ds(pl.multiple_of(t * tk, tk), tk), :]
        s_buf[slot] = lax.dot_general(qs_ref[...], kb, (((1,), (1,)), ((), ())),
                                      preferred_element_type=F32)
        mx = s_buf[slot, :, 0:LANES]
        for c in range(1, tk // LANES):
            mx = jnp.maximum(mx, s_buf[slot, :, c * LANES:(c + 1) * LANES])
        mx_buf[slot] = jnp.broadcast_to(jnp.max(mx, axis=-1, keepdims=True), mx_buf.shape[1:])

    def softmax_pv(t, slot):
        m_new = jnp.maximum(m_ref[...], mx_buf[slot])
        alpha_ref[...] = jnp.exp2(m_ref[...] - m_new)
        m_ref[...] = m_new
        p_buf[slot] = jnp.exp2(s_buf[slot] - jnp.concatenate([m_ref[...]] * (tk // LANES), axis=1)).astype(BF16)
        ones_col = jnp.where(lax.broadcasted_iota(jnp.int32, (tk, LANES), 1) == 0, 1.0, 0.0).astype(BF16)
        vb = jnp.concatenate([v_ref[0, pl.ds(pl.multiple_of(t * tk, tk), tk), :], ones_col], axis=1)
        pv = jnp.dot(p_buf[slot], vb, preferred_element_type=F32)
        acc_ref[...] = jnp.concatenate([alpha_ref[...]] * 2, axis=1) * acc_ref[...] + pv

    scores(0, 0)
    scores(1, 1)

    def body(i, carry):
        for u in range(2):
            softmax_pv(2 * i + u, u)
            scores(2 * i + u + 2, u)
        return carry

    lax.fori_loop(0, n_units // 2 - 1, body, 0)
    softmax_pv(n_units - 2, 0)
    softmax_pv(n_units - 1, 1)

    lam = (jnp.exp(jnp.sum(lq1_ref[...] * lk1_ref[...], axis=-1, keepdims=True))
           - jnp.exp(jnp.sum(lq2_ref[...] * lk2_ref[...], axis=-1, keepdims=True)) + lambda_init)
    a0, a1 = acc_ref[:tq], acc_ref[tq:]
    o = a0[:, :dv] / a0[:, dv:dv + 1] - lam * (a1[:, :dv] / a1[:, dv:dv + 1])
    o = o * lax.rsqrt(jnp.mean(o * o, axis=-1, keepdims=True) + EPS)
    o_ref[0] = (o * (subg_ref[...] * (1.0 - lambda_init))).astype(BF16)


def _diff_attention(qkv, lq1, lk1, lq2, lk2, subg, lambda_init):
    B, S, _ = qkv.shape
    H = DA_HEADS
    tq = min(DA_TQ, S)
    tk = min(DA_TK, S // 2)
    kern = functools.partial(_da_kernel, lambda_init=lambda_init, tk=tk)
    vec = _resident((1, DA_DK))
    return pl.pallas_call(
        kern,
        grid=(B, H, S // tq),
        in_specs=[
            pl.BlockSpec((1, tq, LANES), lambda b, h, i: (b, i, h)),
            pl.BlockSpec((1, S, LANES), lambda b, h, i: (b, 0, H + h)),
            pl.BlockSpec((1, S, LANES), lambda b, h, i: (b, 0, 2 * H + h)),
            vec, vec, vec, vec,
            _resident((1, LANES)),
        ],
        out_specs=pl.BlockSpec((1, tq, LANES), lambda b, h, i: (b, i, h)),
        out_shape=jax.ShapeDtypeStruct((B, S, H * LANES), BF16),
        scratch_shapes=[
            pltpu.VMEM((2 * tq, LANES), BF16),
            pltpu.VMEM((2, 2 * tq, tk), F32),
            pltpu.VMEM((2, 2 * tq, LANES), F32),
            pltpu.VMEM((2, 2 * tq, tk), BF16),
            pltpu.VMEM((2 * tq, LANES), F32),
            pltpu.VMEM((2 * tq, LANES), F32),
            pltpu.VMEM((2 * tq, MXU_DIM), F32),
        ],
        compiler_params=_cparams(("parallel", "parallel", "arbitrary")),
        name="diff_attn",
    )(qkv, qkv, qkv, lq1, lk1, lq2, lk2, subg)


def _wa_kernel(sink_ref, q_ref, kp_ref, kc_ref, kn_ref, vp_ref, vc_ref, vn_ref, o_ref, *, seq_len):
    i = pl.program_id(1)
    tq = q_ref.shape[1]
    nk = tq + 2 * WINDOW
    kcat = jnp.concatenate([kp_ref[0], kc_ref[0], kn_ref[0]], axis=0)
    vcat = jnp.concatenate([vp_ref[0], vc_ref[0], vn_ref[0]], axis=0)
    qpos = i * tq + lax.broadcasted_iota(jnp.int32, (tq, nk), 0)
    kpos = i * tq - WINDOW + lax.broadcasted_iota(jnp.int32, (tq, nk), 1)
    valid = (kpos >= 0) & (kpos < seq_len) & (jnp.abs(qpos - kpos) <= WINDOW)
    lane = lax.broadcasted_iota(jnp.int32, (tq, LANES), 1)
    lo = lane < WA_DH
    even = _in_group_a(lane)
    group = WA_HEADS // WA_KV_HEADS
    for pair in range(WA_HEADS // 2):
        qt = q_ref[0, :, pair * LANES:(pair + 1) * LANES]
        zero = jnp.zeros_like(qt)
        outs = []
        for half in range(2):
            h = 2 * pair + half
            g = h // group
            qh = jnp.where(even, qt, zero) if half == 0 else jnp.where(even, zero, qt)
            kg = kcat[:, g * LANES:(g + 1) * LANES]
            vg = vcat[:, g * LANES:(g + 1) * LANES]
            s = lax.dot_general(qh, kg, (((1,), (1,)), ((), ())), preferred_element_type=F32)
            s = jnp.where(valid, s, -jnp.inf)
            sink = sink_ref[h]
            m = jnp.maximum(jnp.max(s, axis=-1, keepdims=True), sink)
            p = jnp.exp2(s - m)
            denom = jnp.sum(p, axis=-1, keepdims=True) + jnp.exp2(sink - m)
            outs.append(jnp.dot(p.astype(BF16), vg, preferred_element_type=F32) / denom)
        o_ref[0, :, pair * LANES:(pair + 1) * LANES] = jnp.where(lo, outs[0], outs[1]).astype(BF16)


def _window_attention(qkv, sink):
    B, S, _ = qkv.shape
    tq = min(WA_TQ, S)
    qw = WA_HEADS * WA_DH
    kw = WA_KV_HEADS * LANES
    qb, kb, vb = 0, qw // kw, qw // kw + 1
    r = tq // WINDOW
    last = S // WINDOW - 1
    prev = lambda b, i: (b, jnp.maximum(i * r - 1, 0))
    nxt = lambda b, i: (b, jnp.minimum((i + 1) * r, last))
    kern = functools.partial(_wa_kernel, seq_len=S)
    return pl.pallas_call(
        kern,
        grid=(B, S // tq),
        in_specs=[
            pl.BlockSpec(memory_space=pltpu.SMEM),
            pl.BlockSpec((1, tq, qw), lambda b, i: (b, i, qb)),
            pl.BlockSpec((1, WINDOW, kw), lambda b, i: prev(b, i) + (kb,)),
            pl.BlockSpec((1, tq, kw), lambda b, i: (b, i, kb)),
            pl.BlockSpec((1, WINDOW, kw), lambda b, i: nxt(b, i) + (kb,)),
            pl.BlockSpec((1, WINDOW, kw), lambda b, i: prev(b, i) + (vb,)),
            pl.BlockSpec((1, tq, kw), lambda b, i: (b, i, vb)),
            pl.BlockSpec((1, WINDOW, kw), lambda b, i: nxt(b, i) + (vb,)),
        ],
        out_specs=pl.BlockSpec((1, tq, qw), lambda b, i: (b, i, 0)),
        out_shape=jax.ShapeDtypeStruct((B, S, qw), BF16),
        compiler_params=_cparams(("parallel", "parallel")),
        name="window_attn",
    )(sink, qkv, qkv, qkv, qkv, qkv, qkv, qkv)


def _fill_hext(hext, xp_ref, x_ref, xn_ref, gin):
    i = pl.program_id(1)
    tm = x_ref.shape[1]
    hp = _rmsnorm_bf16(xp_ref[0], gin)
    hn = _rmsnorm_bf16(xn_ref[0], gin)
    hext[0:HALO] = jnp.where(i > 0, hp, jnp.zeros_like(hp))
    hext[HALO:HALO + tm] = _rmsnorm_bf16(x_ref[0], gin)
    hext[HALO + tm:] = jnp.where(i < pl.num_programs(1) - 1, hn, jnp.zeros_like(hn))


def _halo_specs(S, tm, D):
    r = tm // HALO
    last = S // HALO - 1
    return [
        pl.BlockSpec((1, HALO, D), lambda b, i: (b, jnp.maximum(i * r - 1, 0), 0)),
        pl.BlockSpec((1, tm, D), lambda b, i: (b, i, 0)),
        pl.BlockSpec((1, HALO, D), lambda b, i: (b, jnp.minimum((i + 1) * r, last), 0)),
    ]


def _shift_rows(y, off, tm):
    if off == 0:
        return y[HALO:HALO + tm]
    return pltpu.roll(y, (-off) % y.shape[0], 0)[HALO:HALO + tm]


def _ffn_kernel(xp_ref, x_ref, xn_ref, gin_ref, wg_ref, wu_ref, cw_ref, wd_ref, o_ref, hext, g_buf, u_buf, acc):
    tm = x_ref.shape[1]
    nc = wg_ref.shape[0]
    _fill_hext(hext, xp_ref, x_ref, xn_ref, gin_ref[...])
    acc[...] = x_ref[0]

    def up(c):
        g_buf[c % 2] = jnp.dot(hext[...], wg_ref[c], preferred_element_type=F32)
        u_buf[c % 2] = jnp.dot(hext[HALO:HALO + tm], wu_ref[c], preferred_element_type=F32)

    def down(c):
        g = g_buf[c % 2]
        cw = cw_ref[c]
        gc = (_shift_rows(g, -1, tm) * cw[0:1] + _shift_rows(g, 0, tm) * cw[1:2]
              + _shift_rows(g, 1, tm) * cw[2:3] + cw[3:4])
        act = (gc * jax.nn.sigmoid(gc) * u_buf[c % 2]).astype(BF16)
        acc[...] += jnp.dot(act, wd_ref[c], preferred_element_type=F32)

    up(0)
    for c in range(nc):
        if c + 1 < nc:
            up(c + 1)
        down(c)
    o_ref[0] = acc[...]


def _conv_ffn(x, gin, wg, wu, cw, wd):
    B, S, D = x.shape
    nc, _, fc = wg.shape
    tm = min(FFN_TILE, S)
    return pl.pallas_call(
        _ffn_kernel,
        grid=(B, S // tm),
        in_specs=_halo_specs(S, tm, D) + [
            _resident((1, D)),
            _resident((nc, D, fc)),
            _resident((nc, D, fc)),
            _resident((nc, 8, fc)),
            _resident((nc, fc, D)),
        ],
        out_specs=pl.BlockSpec((1, tm, D), lambda b, i: (b, i, 0)),
        out_shape=jax.ShapeDtypeStruct((B, S, D), F32),
        scratch_shapes=[pltpu.VMEM((tm + 2 * HALO, D), BF16), pltpu.VMEM((2, tm + 2 * HALO, fc), F32),
                        pltpu.VMEM((2, tm, fc), F32), pltpu.VMEM((tm, D), F32)],
        compiler_params=_cparams(("parallel", "parallel")),
        name="conv_ffn",
    )(x, x, x, gin, wg, wu, cw, wd)


def _rgin_kernel(xp_ref, x_ref, xn_ref, gin_ref, wg_ref, wu_ref, cw_ref, gate_ref, u_ref, hext, *, cn):
    tm = x_ref.shape[1]
    _fill_hext(hext, xp_ref, x_ref, xn_ref, gin_ref[...])
    n_out = wg_ref.shape[1]
    for j in range(n_out // cn):
        cols = slice(j * cn, (j + 1) * cn)
        gate_ref[0, :, cols] = jnp.dot(hext[HALO:HALO + tm], wg_ref[:, cols], preferred_element_type=F32)
        u = jnp.dot(hext[...], wu_ref[:, cols], preferred_element_type=F32)
        cw = cw_ref[:, cols]
        u_ref[0, :, cols] = (_shift_rows(u, -2, tm) * cw[0:1] + _shift_rows(u, -1, tm) * cw[1:2]
                             + _shift_rows(u, 0, tm) * cw[2:3] + _shift_rows(u, 1, tm) * cw[3:4] + cw[4:5])


def _rg_in(x, gin, wg, wu, cw):
    B, S, D = x.shape
    C = wg.shape[1]
    tm = min(ROW_TILE, S)
    kern = functools.partial(_rgin_kernel, cn=512)
    out = jax.ShapeDtypeStruct((B, S, C), F32)
    ospec = pl.BlockSpec((1, tm, C), lambda b, i: (b, i, 0))
    return pl.pallas_call(
        kern,
        grid=(B, S // tm),
        in_specs=_halo_specs(S, tm, D) + [
            _resident((1, D)), _resident((D, C)), _resident((D, C)), _resident((8, C))],
        out_specs=[ospec, ospec],
        out_shape=[out, out],
        scratch_shapes=[pltpu.VMEM((tm + 2 * HALO, D), BF16)],
        compiler_params=_cparams(("parallel", "parallel")),
        name="rg_in",
    )(x, x, x, gin, wg, wu, cw)


def _rg_gate_tiles(C, bw):
    tiles = []
    for j in range(C // MXU_DIM):
        b_lo = (j * MXU_DIM) // bw
        b_hi = (j * MXU_DIM + MXU_DIM - 1) // bw
        k_lo = (b_lo * bw) // LANES * LANES
        k_hi = min(C, -(-((b_hi + 1) * bw) // LANES) * LANES)
        tiles.append((k_lo, k_hi))
    return tiles


def _rgscan_kernel(*refs, reverse, final, bw):
    if final:
        (u_ref, wr_ref, wi_ref, gb_ref, hf_ref, gate_ref, x_ref, wo_ref, o_ref,
         a_s, b_s, h_s, carry) = refs
    else:
        u_ref, wr_ref, wi_ref, gb_ref, o_ref, a_s, b_s, h_s, carry = refs
    tc, C = u_ref.shape[1], u_ref.shape[2]

    @pl.when(pl.program_id(1) == 0)
    def _init():
        carry[...] = jnp.zeros(carry.shape, F32)

    u = u_ref[0]
    ub = u.astype(BF16)
    for j, (k_lo, k_hi) in enumerate(_rg_gate_tiles(C, bw)):
        cols = slice(j * MXU_DIM, (j + 1) * MXU_DIM)
        rp = jnp.dot(ub[:, k_lo:k_hi], wr_ref[k_lo:k_hi, cols], preferred_element_type=F32) + gb_ref[0:1, cols]
        ip = jnp.dot(ub[:, k_lo:k_hi], wi_ref[k_lo:k_hi, cols], preferred_element_type=F32) + gb_ref[1:2, cols]
        log_a = (-RG_C * jax.nn.softplus(-gb_ref[2:3, cols])) * _sigmoid(rp)
        a = jnp.exp(log_a)
        a_s[:, cols] = a
        b_s[:, cols] = jnp.sqrt(-jnp.tanh(log_a) * (a * a + 1.0)) * (_sigmoid(ip) * u[:, cols])

    ng = tc // 8

    def body(gi, h):
        g = ng - 1 - gi if reverse else gi
        r0 = pl.multiple_of(g * 8, 8)
        a8 = a_s[pl.ds(r0, 8), :]
        b8 = b_s[pl.ds(r0, 8), :]
        rows = [None] * 8
        for t in (range(7, -1, -1) if reverse else range(8)):
            h = a8[t:t + 1] * h + b8[t:t + 1]
            rows[t] = h
        h_s[pl.ds(r0, 8), :] = jnp.concatenate(rows, axis=0)
        return h

    carry[0:1, :] = lax.fori_loop(0, ng, body, carry[0:1, :])
    if final:
        y = ((hf_ref[0] + h_s[...]) * jax.nn.gelu(gate_ref[0])).astype(BF16)
        o_ref[0] = x_ref[0] + jnp.dot(y, wo_ref[...], preferred_element_type=F32)
    else:
        o_ref[0] = h_s[...]


def _rg_scan(u, wr, wi, gb, *, reverse, extra=None):
    B, S, C = u.shape
    tc = min(RG_TC, S)
    n = S // tc
    idx = (lambda b, i: (b, n - 1 - i, 0)) if reverse else (lambda b, i: (b, i, 0))
    final = extra is not None
    in_specs = [pl.BlockSpec((1, tc, C), idx), _resident((C, C)), _resident((C, C)), _resident((8, C))]
    args = [u, wr, wi, gb]
    out_w = C
    if final:
        hf, gate, x, wo = extra
        out_w = x.shape[2]
        in_specs += [pl.BlockSpec((1, tc, C), idx), pl.BlockSpec((1, tc, C), idx),
                     pl.BlockSpec((1, tc, out_w), idx), _resident(wo.shape)]
        args += [hf, gate, x, wo]
    kern = functools.partial(_rgscan_kernel, reverse=reverse, final=final, bw=C // RG_BLOCKS)
    return pl.pallas_call(
        kern,
        grid=(B, n),
        in_specs=in_specs,
        out_specs=pl.BlockSpec((1, tc, out_w), idx),
        out_shape=jax.ShapeDtypeStruct((B, S, out_w), F32),
        scratch_shapes=[pltpu.VMEM((tc, C), F32), pltpu.VMEM((tc, C), F32), pltpu.VMEM((tc, C), F32),
                        pltpu.VMEM((8, C), F32)],
        compiler_params=_cparams(("parallel", "arbitrary")),
        name="rg_scan_final" if final else "rg_scan",
    )(*args)


def _rope_tables(S, dh):
    rot = dh // ROT_FRAC
    half = rot // 2
    inv = ROPE_THETA ** (-jnp.arange(half, dtype=F32) * 2.0 / rot)
    ang = jnp.arange(S, dtype=F32)[:, None] * inv[None, :]
    cos, sin = jnp.cos(ang), jnp.sin(ang)
    rest = dh - rot
    c = jnp.concatenate([cos, cos, jnp.ones((S, rest), F32)], axis=1)
    s = jnp.concatenate([-sin, sin, jnp.zeros((S, rest), F32)], axis=1)
    reps = LANES // dh
    return _rope_layout(jnp.tile(c, (1, reps)), 1), _rope_layout(jnp.tile(s, (1, reps)), 1)


def _block_diag(w):
    n, c, e = w.shape
    eye = jnp.eye(n, dtype=w.dtype)
    return (w[:, :, None, :] * eye[:, None, :, None]).reshape(n * c, n * e)


def _prepare(p):
    depth, D = p['norm_mix'].shape
    F = p['ffn_w_down'].shape[1]
    fc = MXU_DIM
    nc = F // fc
    prep = {'norm_mix': p['norm_mix'].reshape(depth, 1, D), 'norm_ffn': p['norm_ffn'].reshape(depth, 1, D)}
    wup = p['ffn_w_up'].astype(BF16)
    prep['ffn_wg'] = wup[:, :, :F].reshape(depth, D, nc, fc).transpose(0, 2, 1, 3)
    prep['ffn_wu'] = wup[:, :, F:].reshape(depth, D, nc, fc).transpose(0, 2, 1, 3)
    cw = jnp.concatenate([p['ffn_conv_w'], p['ffn_conv_b'][:, None, :]], axis=1)
    cw = jnp.pad(cw, ((0, 0), (0, 8 - cw.shape[1]), (0, 0)))
    prep['ffn_cw'] = cw.reshape(depth, 8, nc, fc).transpose(0, 2, 1, 3)
    prep['ffn_wd'] = p['ffn_w_down'].astype(BF16).reshape(depth, nc, fc, D)

    n_da_q = DA_HEADS * 2
    prep['da_w_qkv'] = _rope_layout(p['da_w_qkv'], n_da_q).astype(BF16)
    da_geff = jnp.concatenate([jnp.tile(p['da_q_norm'] * (DA_DK ** -0.5 * LOG2E), (1, n_da_q)),
                               jnp.tile(p['da_k_norm'], (1, n_da_q))], axis=1)
    prep['da_geff'] = _rope_layout(da_geff, n_da_q)[:, None, :]
    prep['da_w_o'] = p['da_w_o'].astype(BF16)
    for name in ('da_lambda_q1', 'da_lambda_k1', 'da_lambda_q2', 'da_lambda_k2', 'da_sub_norm'):
        prep[name] = p[name][:, None, :]

    qw, kvw = WA_HEADS * WA_DH, WA_KV_HEADS * WA_DH
    w = p['wa_w_qkv']
    n_b = w.shape[0]

    def dup(cols):
        c = cols.reshape(n_b, D, WA_KV_HEADS, 1, WA_DH)
        return jnp.broadcast_to(c, (n_b, D, WA_KV_HEADS, 2, WA_DH)).reshape(n_b, D, 2 * kvw)

    n_wa_qk = (qw + 2 * kvw) // LANES
    wa_w = jnp.concatenate([w[:, :, :qw], dup(w[:, :, qw:qw + kvw]), dup(w[:, :, qw + kvw:])], axis=2)
    prep['wa_w_qkv'] = _rope_layout(wa_w, n_wa_qk).astype(BF16)
    wa_geff = jnp.concatenate([jnp.tile(p['wa_q_norm'] * (WA_DH ** -0.5 * LOG2E), (1, WA_HEADS)),
                               jnp.tile(p['wa_k_norm'], (1, 2 * WA_KV_HEADS))], axis=1)
    prep['wa_geff'] = _rope_layout(wa_geff, n_wa_qk)[:, None, :]
    prep['wa_sink'] = p['wa_sink'] * LOG2E
    prep['wa_w_o'] = p['wa_w_o'].astype(BF16)

    C = p['rg_w_out'].shape[1]
    bw = C // RG_BLOCKS
    win = p['rg_w_in'].astype(BF16)
    prep['rg_wg'], prep['rg_wu'] = win[:, :, :C], win[:, :, C:]
    rcw = jnp.concatenate([p['rg_conv_w'], p['rg_conv_b'][:, None, :]], axis=1)
    prep['rg_cw'] = jnp.pad(rcw, ((0, 0), (0, 8 - rcw.shape[1]), (0, 0)))
    gw, gb = p['rg_gate_w'], p['rg_gate_b']
    n_c = gw.shape[0]
    prep['rg_wr'] = jnp.stack([jnp.stack([_block_diag(gw[j, d, :, :, :bw]) for d in range(2)])
                               for j in range(n_c)]).astype(BF16)
    prep['rg_wi'] = jnp.stack([jnp.stack([_block_diag(gw[j, d, :, :, bw:]) for d in range(2)])
                               for j in range(n_c)]).astype(BF16)
    gvec = jnp.stack([gb[..., :bw].reshape(n_c, 2, C), gb[..., bw:].reshape(n_c, 2, C), p['rg_lambda']], axis=2)
    prep['rg_gb'] = jnp.pad(gvec, ((0, 0), (0, 0), (0, 5), (0, 0)))
    prep['rg_w_out'] = p['rg_w_out'].astype(BF16)
    return prep


def _lambda_init(layer_idx):
    return 0.8 - 0.6 * math.exp(-0.3 * layer_idx)


def _trunk(x, w):
    S = x.shape[1]
    cos, sin = _rope_tables(S, DA_DK)
    depth = w['norm_mix'].shape[0]
    for i in range(depth):
        kind, j = i % N_MIXERS, i // N_MIXERS
        gin = w['norm_mix'][i]
        if kind == 0:
            qkv = _project(x, gin, w['da_w_qkv'][j], w['da_geff'][j], cos, sin, 2 * DA_HEADS * LANES)
            o = _diff_attention(qkv, w['da_lambda_q1'][j], w['da_lambda_k1'][j], w['da_lambda_q2'][j],
                                w['da_lambda_k2'][j], w['da_sub_norm'][j], _lambda_init(i))
            x = _outproj(o, w['da_w_o'][j], x)
        elif kind == 1:
            n_prep = WA_HEADS * WA_DH + WA_KV_HEADS * LANES
            qkv = _project(x, gin, w['wa_w_qkv'][j], w['wa_geff'][j], cos, sin, n_prep)
            o = _window_attention(qkv, w['wa_sink'][j])
            x = _outproj(o, w['wa_w_o'][j], x)
        else:
            gate, u = _rg_in(x, gin, w['rg_wg'][j], w['rg_wu'][j], w['rg_cw'][j])
            hf = _rg_scan(u, w['rg_wr'][j, 0], w['rg_wi'][j, 0], w['rg_gb'][j, 0], reverse=False)
            x = _rg_scan(u, w['rg_wr'][j, 1], w['rg_wi'][j, 1], w['rg_gb'][j, 1], reverse=True,
                         extra=(hf, gate, x, w['rg_w_out'][j]))
        x = _conv_ffn(x, w['norm_ffn'][i], w['ffn_wg'][i], w['ffn_wu'][i], w['ffn_cw'][i], w['ffn_wd'][i])
    return x


def kernel(x_prompt, x_sample, norm_mix, norm_ffn, ffn_w_up, ffn_conv_w, ffn_conv_b, ffn_w_down, da_w_qkv, da_q_norm, da_k_norm, da_lambda_q1, da_lambda_k1, da_lambda_q2, da_lambda_k2, da_sub_norm, da_w_o, wa_w_qkv, wa_q_norm, wa_k_norm, wa_sink, wa_w_o, rg_w_in, rg_conv_w, rg_conv_b, rg_gate_w, rg_gate_b, rg_lambda, rg_w_out):
    w = _prepare(dict(
        norm_mix=norm_mix, norm_ffn=norm_ffn, ffn_w_up=ffn_w_up, ffn_conv_w=ffn_conv_w, ffn_conv_b=ffn_conv_b,
        ffn_w_down=ffn_w_down, da_w_qkv=da_w_qkv, da_q_norm=da_q_norm, da_k_norm=da_k_norm,
        da_lambda_q1=da_lambda_q1, da_lambda_k1=da_lambda_k1, da_lambda_q2=da_lambda_q2,
        da_lambda_k2=da_lambda_k2, da_sub_norm=da_sub_norm, da_w_o=da_w_o, wa_w_qkv=wa_w_qkv,
        wa_q_norm=wa_q_norm, wa_k_norm=wa_k_norm, wa_sink=wa_sink, wa_w_o=wa_w_o, rg_w_in=rg_w_in,
        rg_conv_w=rg_conv_w, rg_conv_b=rg_conv_b, rg_gate_w=rg_gate_w, rg_gate_b=rg_gate_b,
        rg_lambda=rg_lambda, rg_w_out=rg_w_out))
    return (_trunk(x_prompt, w), _trunk(x_sample, w))
```

```python
import functools
import math

import jax
import jax.numpy as jnp
from jax import lax
from jax.experimental import pallas as pl
from jax.experimental.pallas import tpu as pltpu

F32 = jnp.float32
BF16 = jnp.bfloat16

DA_HEADS = 8
DA_DK = 64
WA_HEADS = 16
WA_KV_HEADS = 4
WA_DH = 64
WINDOW = 128
RG_BLOCKS = 16
RG_C = 8.0
ROPE_THETA = 500000.0
ROT_FRAC = 4
EPS = 1e-6
N_MIXERS = 3
LOG2E = math.log2(math.e)

LANES = 128
MXU_DIM = 256
HALO = 16
ROW_TILE = 512
FFN_TILE = 1024
DA_TQ, DA_TK = 512, 1024
WA_TQ = 256
RG_TC = 256
VMEM_LIMIT = 56 * 1024 * 1024


def _cparams(sem):
    return pltpu.CompilerParams(dimension_semantics=sem, vmem_limit_bytes=VMEM_LIMIT)


def _rmsnorm_bf16(x, g):
    return (x * lax.rsqrt(jnp.mean(x * x, axis=-1, keepdims=True) + EPS) * g).astype(BF16)


def _sigmoid(x):
    return 0.5 * jnp.tanh(0.5 * x) + 0.5


def _in_group_a(lane):
    return (lane < 8) | ((lane >= 16) & (lane < 72))


def _rope_layout(a, n_tiles):
    lead = a.shape[:-1]
    head = a[..., :n_tiles * LANES].reshape(lead + (n_tiles, LANES))
    head = jnp.concatenate([head[..., 0:8], head[..., 64:72], head[..., 16:64],
                            head[..., 8:16], head[..., 72:LANES]], axis=-1)
    return jnp.concatenate([head.reshape(lead + (n_tiles * LANES,)), a[..., n_tiles * LANES:]], axis=-1)


def _resident(shape):
    nd = len(shape)
    return pl.BlockSpec(shape, lambda *_: (0,) * nd, pipeline_mode=pl.Buffered(1))


def _proj_kernel(x_ref, gin_ref, w_ref, geff_ref, cos_ref, sin_ref, o_ref, *, n_prep, cn):
    tm = x_ref.shape[1]
    n_out = w_ref.shape[1]
    h = _rmsnorm_bf16(x_ref[0], gin_ref[...])
    lo = _in_group_a(lax.broadcasted_iota(jnp.int32, (tm, LANES), 1))
    cos = cos_ref[...]
    sin = sin_ref[...]
    for j in range(n_out // cn):
        y = jnp.dot(h, w_ref[:, j * cn:(j + 1) * cn], preferred_element_type=F32)
        for t in range(cn // LANES):
            col = j * cn + t * LANES
            yt = y[:, t * LANES:(t + 1) * LANES]
            if col < n_prep:
                sq = yt * yt
                s_lo = jnp.sum(jnp.where(lo, sq, 0.0), axis=-1, keepdims=True)
                s_hi = jnp.sum(jnp.where(lo, 0.0, sq), axis=-1, keepdims=True)
                r = lax.rsqrt(jnp.where(lo, s_lo, s_hi) * (1.0 / 64.0) + EPS)
                yn = yt * r * geff_ref[:, col:col + LANES]
                yt = yn * cos + pltpu.roll(yn, LANES // 2, 1) * sin
            o_ref[0, :, col:col + LANES] = yt.astype(BF16)


def _project(x, gin, w, geff, cos, sin, n_prep):
    B, S, D = x.shape
    n_out = w.shape[1]
    tm = min(ROW_TILE, S)
    kern = functools.partial(_proj_kernel, n_prep=n_prep, cn=512)
    return pl.pallas_call(
        kern,
        grid=(B, S // tm),
        in_specs=[
            pl.BlockSpec((1, tm, D), lambda b, i: (b, i, 0)),
            _resident((1, D)),
            _resident((D, n_out)),
            _resident((1, n_prep)),
            pl.BlockSpec((tm, LANES), lambda b, i: (i, 0)),
            pl.BlockSpec((tm, LANES), lambda b, i: (i, 0)),
        ],
        out_specs=pl.BlockSpec((1, tm, n_out), lambda b, i: (b, i, 0)),
        out_shape=jax.ShapeDtypeStruct((B, S, n_out), BF16),
        compiler_params=_cparams(("parallel", "parallel")),
        name="proj",
    )(x, gin, w, geff, cos, sin)


def _outproj_kernel(a_ref, w_ref, x_ref, o_ref):
    o_ref[0] = x_ref[0] + jnp.dot(a_ref[0], w_ref[...], preferred_element_type=F32)


def _outproj(a, w, x):
    B, S, K = a.shape
    D = w.shape[1]
    tm = min(ROW_TILE, S)
    return pl.pallas_call(
        _outproj_kernel,
        grid=(B, S // tm),
        in_specs=[
            pl.BlockSpec((1, tm, K), lambda b, i: (b, i, 0)),
            _resident((K, D)),
            pl.BlockSpec((1, tm, D), lambda b, i: (b, i, 0)),
        ],
        out_specs=pl.BlockSpec((1, tm, D), lambda b, i: (b, i, 0)),
        out_shape=jax.ShapeDtypeStruct((B, S, D), F32),
        compiler_params=_cparams(("parallel", "parallel")),
        name="outproj",
    )(a, w, x)


def _da_kernel(q_ref, k_ref, v_ref, lq1_ref, lk1_ref, lq2_ref, lk2_ref, subg_ref, o_ref,
               qs_ref, s_buf, mx_buf, p_buf, m_ref, alpha_ref, acc_ref, *, lambda_init, tk):
    tq = q_ref.shape[1]
    n_units = k_ref.shape[1] // tk
    dv = LANES

    q = q_ref[0]
    map0 = _in_group_a(lax.broadcasted_iota(jnp.int32, q.shape, 1))
    zero = jnp.zeros_like(q)
    qs_ref[:tq] = jnp.where(map0, q, zero)
    qs_ref[tq:] = jnp.where(map0, zero, q)
    m_ref[...] = jnp.full(m_ref.shape, -jnp.inf, F32)
    acc_ref[...] = jnp.zeros(acc_ref.shape, F32)

    def scores(t, slot):
        kb = k_ref[0, pl.ds(pl.multiple_of(t * tk, tk), tk), :]
        s_buf[slot] = lax.dot_general(qs_ref[...], kb, (((1,), (1,)), ((), ())),
                                      preferred_element_type=F32)
        mx = s_buf[slot, :, 0:LANES]
        for c in range(1, tk // LANES):
            mx = jnp.maximum(mx, s_buf[slot, :, c * LANES:(c + 1) * LANES])
        mx_buf[slot] = jnp.broadcast_to(jnp.max(mx, axis=-1, keepdims=True), mx_buf.shape[1:])

    def softmax_pv(t, slot):
        m_new = jnp.maximum(m_ref[...], mx_buf[slot])
        alpha_ref[...] = jnp.exp2(m_ref[...] - m_new)
        m_ref[...] = m_new
        p_buf[slot] = jnp.exp2(s_buf[slot] - jnp.concatenate([m_ref[...]] * (tk // LANES), axis=1)).astype(BF16)
        ones_col = jnp.where(lax.broadcasted_iota(jnp.int32, (tk, LANES), 1) == 0, 1.0, 0.0).astype(BF16)
        vb = jnp.concatenate([v_ref[0, pl.ds(pl.multiple_of(t * tk, tk), tk), :], ones_col], axis=1)
        pv = jnp.dot(p_buf[slot], vb, preferred_element_type=F32)
        acc_ref[...] = jnp.concatenate([alpha_ref[...]] * 2, axis=1) * acc_ref[...] + pv

    scores(0, 0)
    scores(1, 1)

    def body(i, carry):
        for u in range(2):
            softmax_pv(2 * i + u, u)
            scores(2 * i + u + 2, u)
        return carry

    lax.fori_loop(0, n_units // 2 - 1, body, 0)
    softmax_pv(n_units - 2, 0)
    softmax_pv(n_units - 1, 1)

    lam = (jnp.exp(jnp.sum(lq1_ref[...] * lk1_ref[...], axis=-1, keepdims=True))
           - jnp.exp(jnp.sum(lq2_ref[...] * lk2_ref[...], axis=-1, keepdims=True)) + lambda_init)
    a0, a1 = acc_ref[:tq], acc_ref[tq:]
    o = a0[:, :dv] / a0[:, dv:dv + 1] - lam * (a1[:, :dv] / a1[:, dv:dv + 1])
    o = o * lax.rsqrt(jnp.mean(o * o, axis=-1, keepdims=True) + EPS)
    o_ref[0] = (o * (subg_ref[...] * (1.0 - lambda_init))).astype(BF16)


def _diff_attention(qkv, lq1, lk1, lq2, lk2, subg, lambda_init):
    B, S, _ = qkv.shape
    H = DA_HEADS
    tq = min(DA_TQ, S)
    tk = min(DA_TK, S // 2)
    kern = functools.partial(_da_kernel, lambda_init=lambda_init, tk=tk)
    vec = _resident((1, DA_DK))
    return pl.pallas_call(
        kern,
        grid=(B, H, S // tq),
        in_specs=[
            pl.BlockSpec((1, tq, LANES), lambda b, h, i: (b, i, h)),
            pl.BlockSpec((1, S, LANES), lambda b, h, i: (b, 0, H + h)),
            pl.BlockSpec((1, S, LANES), lambda b, h, i: (b, 0, 2 * H + h)),
            vec, vec, vec, vec,
            _resident((1, LANES)),
        ],
        out_specs=pl.BlockSpec((1, tq, LANES), lambda b, h, i: (b, i, h)),
        out_shape=jax.ShapeDtypeStruct((B, S, H * LANES), BF16),
        scratch_shapes=[
            pltpu.VMEM((2 * tq, LANES), BF16),
            pltpu.VMEM((2, 2 * tq, tk), F32),
            pltpu.VMEM((2, 2 * tq, LANES), F32),
            pltpu.VMEM((2, 2 * tq, tk), BF16),
            pltpu.VMEM((2 * tq, LANES), F32),
            pltpu.VMEM((2 * tq, LANES), F32),
            pltpu.VMEM((2 * tq, MXU_DIM), F32),
        ],
        compiler_params=_cparams(("parallel", "parallel", "arbitrary")),
        name="diff_attn",
    )(qkv, qkv, qkv, lq1, lk1, lq2, lk2, subg)


def _wa_kernel(sink_ref, q_ref, kp_ref, kc_ref, kn_ref, vp_ref, vc_ref, vn_ref, o_ref, *, seq_len):
    i = pl.program_id(1)
    tq = q_ref.shape[1]
    nk = tq + 2 * WINDOW
    kcat = jnp.concatenate([kp_ref[0], kc_ref[0], kn_ref[0]], axis=0)
    vcat = jnp.concatenate([vp_ref[0], vc_ref[0], vn_ref[0]], axis=0)
    qpos = i * tq + lax.broadcasted_iota(jnp.int32, (tq, nk), 0)
    kpos = i * tq - WINDOW + lax.broadcasted_iota(jnp.int32, (tq, nk), 1)
    valid = (kpos >= 0) & (kpos < seq_len) & (jnp.abs(qpos - kpos) <= WINDOW)
    lane = lax.broadcasted_iota(jnp.int32, (tq, LANES), 1)
    lo = lane < WA_DH
    even = _in_group_a(lane)
    group = WA_HEADS // WA_KV_HEADS
    for pair in range(WA_HEADS // 2):
        qt = q_ref[0, :, pair * LANES:(pair + 1) * LANES]
        zero = jnp.zeros_like(qt)
        outs = []
        for half in range(2):
            h = 2 * pair + half
            g = h // group
            qh = jnp.where(even, qt, zero) if half == 0 else jnp.where(even, zero, qt)
            kg = kcat[:, g * LANES:(g + 1) * LANES]
            vg = vcat[:, g * LANES:(g + 1) * LANES]
            s = lax.dot_general(qh, kg, (((1,), (1,)), ((), ())), preferred_element_type=F32)
            s = jnp.where(valid, s, -jnp.inf)
            sink = sink_ref[h]
            m = jnp.maximum(jnp.max(s, axis=-1, keepdims=True), sink)
            p = jnp.exp2(s - m)
            denom = jnp.sum(p, axis=-1, keepdims=True) + jnp.exp2(sink - m)
            outs.append(jnp.dot(p.astype(BF16), vg, preferred_element_type=F32) / denom)
        o_ref[0, :, pair * LANES:(pair + 1) * LANES] = jnp.where(lo, outs[0], outs[1]).astype(BF16)


def _window_attention(qkv, sink):
    B, S, _ = qkv.shape
    tq = min(WA_TQ, S)
    qw = WA_HEADS * WA_DH
    kw = WA_KV_HEADS * LANES
    qb, kb, vb = 0, qw // kw, qw // kw + 1
    r = tq // WINDOW
    last = S // WINDOW - 1
    prev = lambda b, i: (b, jnp.maximum(i * r - 1, 0))
    nxt = lambda b, i: (b, jnp.minimum((i + 1) * r, last))
    kern = functools.partial(_wa_kernel, seq_len=S)
    return pl.pallas_call(
        kern,
        grid=(B, S // tq),
        in_specs=[
            pl.BlockSpec(memory_space=pltpu.SMEM),
            pl.BlockSpec((1, tq, qw), lambda b, i: (b, i, qb)),
            pl.BlockSpec((1, WINDOW, kw), lambda b, i: prev(b, i) + (kb,)),
            pl.BlockSpec((1, tq, kw), lambda b, i: (b, i, kb)),
            pl.BlockSpec((1, WINDOW, kw), lambda b, i: nxt(b, i) + (kb,)),
            pl.BlockSpec((1, WINDOW, kw), lambda b, i: prev(b, i) + (vb,)),
            pl.BlockSpec((1, tq, kw), lambda b, i: (b, i, vb)),
            pl.BlockSpec((1, WINDOW, kw), lambda b, i: nxt(b, i) + (vb,)),
        ],
        out_specs=pl.BlockSpec((1, tq, qw), lambda b, i: (b, i, 0)),
        out_shape=jax.ShapeDtypeStruct((B, S, qw), BF16),
        compiler_params=_cparams(("parallel", "parallel")),
        name="window_attn",
    )(sink, qkv, qkv, qkv, qkv, qkv, qkv, qkv)


def _fill_hext(hext, xp_ref, x_ref, xn_ref, gin):
    i = pl.program_id(1)
    tm = x_ref.shape[1]
    hp = _rmsnorm_bf16(xp_ref[0], gin)
    hn = _rmsnorm_bf16(xn_ref[0], gin)
    hext[0:HALO] = jnp.where(i > 0, hp, jnp.zeros_like(hp))
    hext[HALO:HALO + tm] = _rmsnorm_bf16(x_ref[0], gin)
    hext[HALO + tm:] = jnp.where(i < pl.num_programs(1) - 1, hn, jnp.zeros_like(hn))


def _halo_specs(S, tm, D):
    r = tm // HALO
    last = S // HALO - 1
    return [
        pl.BlockSpec((1, HALO, D), lambda b, i: (b, jnp.maximum(i * r - 1, 0), 0)),
        pl.BlockSpec((1, tm, D), lambda b, i: (b, i, 0)),
        pl.BlockSpec((1, HALO, D), lambda b, i: (b, jnp.minimum((i + 1) * r, last), 0)),
    ]


def _shift_rows(y, off, tm):
    if off == 0:
        return y[HALO:HALO + tm]
    return pltpu.roll(y, (-off) % y.shape[0], 0)[HALO:HALO + tm]


def _ffn_kernel(xp_ref, x_ref, xn_ref, gin_ref, wg_ref, wu_ref, cw_ref, wd_ref, o_ref, hext, g_buf, u_buf, acc):
    tm = x_ref.shape[1]
    nc = wg_ref.shape[0]
    _fill_hext(hext, xp_ref, x_ref, xn_ref, gin_ref[...])
    acc[...] = x_ref[0]

    def up(c):
        g_buf[c % 2] = jnp.dot(hext[...], wg_ref[c], preferred_element_type=F32)
        u_buf[c % 2] = jnp.dot(hext[HALO:HALO + tm], wu_ref[c], preferred_element_type=F32)

    def down(c):
        g = g_buf[c % 2]
        cw = cw_ref[c]
        gc = (_shift_rows(g, -1, tm) * cw[0:1] + _shift_rows(g, 0, tm) * cw[1:2]
              + _shift_rows(g, 1, tm) * cw[2:3] + cw[3:4])
        act = (gc * jax.nn.sigmoid(gc) * u_buf[c % 2]).astype(BF16)
        acc[...] += jnp.dot(act, wd_ref[c], preferred_element_type=F32)

    up(0)
    for c in range(nc):
        if c + 1 < nc:
            up(c + 1)
        down(c)
    o_ref[0] = acc[...]


def _conv_ffn(x, gin, wg, wu, cw, wd):
    B, S, D = x.shape
    nc, _, fc = wg.shape
    tm = min(FFN_TILE, S)
    return pl.pallas_call(
        _ffn_kernel,
        grid=(B, S // tm),
        in_specs=_halo_specs(S, tm, D) + [
            _resident((1, D)),
            _resident((nc, D, fc)),
            _resident((nc, D, fc)),
            _resident((nc, 8, fc)),
            _resident((nc, fc, D)),
        ],
        out_specs=pl.BlockSpec((1, tm, D), lambda b, i: (b, i, 0)),
        out_shape=jax.ShapeDtypeStruct((B, S, D), F32),
        scratch_shapes=[pltpu.VMEM((tm + 2 * HALO, D), BF16), pltpu.VMEM((2, tm + 2 * HALO, fc), F32),
                        pltpu.VMEM((2, tm, fc), F32), pltpu.VMEM((tm, D), F32)],
        compiler_params=_cparams(("parallel", "parallel")),
        name="conv_ffn",
    )(x, x, x, gin, wg, wu, cw, wd)


def _rgin_kernel(xp_ref, x_ref, xn_ref, gin_ref, wg_ref, wu_ref, cw_ref, gate_ref, u_ref, hext, *, cn):
    tm = x_ref.shape[1]
    _fill_hext(hext, xp_ref, x_ref, xn_ref, gin_ref[...])
    n_out = wg_ref.shape[1]
    for j in range(n_out // cn):
        cols = slice(j * cn, (j + 1) * cn)
        gate_ref[0, :, cols] = jnp.dot(hext[HALO:HALO + tm], wg_ref[:, cols], preferred_element_type=F32)
        u = jnp.dot(hext[...], wu_ref[:, cols], preferred_element_type=F32)
        cw = cw_ref[:, cols]
        u_ref[0, :, cols] = (_shift_rows(u, -2, tm) * cw[0:1] + _shift_rows(u, -1, tm) * cw[1:2]
                             + _shift_rows(u, 0, tm) * cw[2:3] + _shift_rows(u, 1, tm) * cw[3:4] + cw[4:5])


def _rg_in(x, gin, wg, wu, cw):
    B, S, D = x.shape
    C = wg.shape[1]
    tm = min(ROW_TILE, S)
    kern = functools.partial(_rgin_kernel, cn=512)
    out = jax.ShapeDtypeStruct((B, S, C), F32)
    ospec = pl.BlockSpec((1, tm, C), lambda b, i: (b, i, 0))
    return pl.pallas_call(
        kern,
        grid=(B, S // tm),
        in_specs=_halo_specs(S, tm, D) + [
            _resident((1, D)), _resident((D, C)), _resident((D, C)), _resident((8, C))],
        out_specs=[ospec, ospec],
        out_shape=[out, out],
        scratch_shapes=[pltpu.VMEM((tm + 2 * HALO, D), BF16)],
        compiler_params=_cparams(("parallel", "parallel")),
        name="rg_in",
    )(x, x, x, gin, wg, wu, cw)


def _rg_gate_tiles(C, bw):
    tiles = []
    for j in range(C // MXU_DIM):
        b_lo = (j * MXU_DIM) // bw
        b_hi = (j * MXU_DIM + MXU_DIM - 1) // bw
        k_lo = (b_lo * bw) // LANES * LANES
        k_hi = min(C, -(-((b_hi + 1) * bw) // LANES) * LANES)
        tiles.append((k_lo, k_hi))
    return tiles


def _rgscan_kernel(*refs, reverse, final, bw):
    if final:
        (u_ref, wr_ref, wi_ref, gb_ref, hf_ref, gate_ref, x_ref, wo_ref, o_ref,
         a_s, b_s, h_s, carry) = refs
    else:
        u_ref, wr_ref, wi_ref, gb_ref, o_ref, a_s, b_s, h_s, carry = refs
    tc, C = u_ref.shape[1], u_ref.shape[2]

    @pl.when(pl.program_id(1) == 0)
    def _init():
        carry[...] = jnp.zeros(carry.shape, F32)

    u = u_ref[0]
    ub = u.astype(BF16)
    for j, (k_lo, k_hi) in enumerate(_rg_gate_tiles(C, bw)):
        cols = slice(j * MXU_DIM, (j + 1) * MXU_DIM)
        rp = jnp.dot(ub[:, k_lo:k_hi], wr_ref[k_lo:k_hi, cols], preferred_element_type=F32) + gb_ref[0:1, cols]
        ip = jnp.dot(ub[:, k_lo:k_hi], wi_ref[k_lo:k_hi, cols], preferred_element_type=F32) + gb_ref[1:2, cols]
        log_a = (-RG_C * jax.nn.softplus(-gb_ref[2:3, cols])) * _sigmoid(rp)
        a = jnp.exp(log_a)
        a_s[:, cols] = a
        b_s[:, cols] = jnp.sqrt(-jnp.tanh(log_a) * (a * a + 1.0)) * (_sigmoid(ip) * u[:, cols])

    ng = tc // 8

    def body(gi, h):
        g = ng - 1 - gi if reverse else gi
        r0 = pl.multiple_of(g * 8, 8)
        for t in (range(7, -1, -1) if reverse else range(8)):
            h = a_s[pl.ds(r0 + t, 1), :] * h + b_s[pl.ds(r0 + t, 1), :]
            h_s[pl.ds(r0 + t, 1), :] = h
        return h

    carry[0:1, :] = lax.fori_loop(0, ng, body, carry[0:1, :])
    if final:
        y = ((hf_ref[0] + h_s[...]) * jax.nn.gelu(gate_ref[0])).astype(BF16)
        o_ref[0] = x_ref[0] + jnp.dot(y, wo_ref[...], preferred_element_type=F32)
    else:
        o_ref[0] = h_s[...]


def _rg_scan(u, wr, wi, gb, *, reverse, extra=None):
    B, S, C = u.shape
    tc = min(RG_TC, S)
    n = S // tc
    idx = (lambda b, i: (b, n - 1 - i, 0)) if reverse else (lambda b, i: (b, i, 0))
    final = extra is not None
    in_specs = [pl.BlockSpec((1, tc, C), idx), _resident((C, C)), _resident((C, C)), _resident((8, C))]
    args = [u, wr, wi, gb]
    out_w = C
    if final:
        hf, gate, x, wo = extra
        out_w = x.shape[2]
        in_specs += [pl.BlockSpec((1, tc, C), idx), pl.BlockSpec((1, tc, C), idx),
                     pl.BlockSpec((1, tc, out_w), idx), _resident(wo.shape)]
        args += [hf, gate, x, wo]
    kern = functools.partial(_rgscan_kernel, reverse=reverse, final=final, bw=C // RG_BLOCKS)
    return pl.pallas_call(
        kern,
        grid=(B, n),
        in_specs=in_specs,
        out_specs=pl.BlockSpec((1, tc, out_w), idx),
        out_shape=jax.ShapeDtypeStruct((B, S, out_w), F32),
        scratch_shapes=[pltpu.VMEM((tc, C), F32), pltpu.VMEM((tc, C), F32), pltpu.VMEM((tc, C), F32),
                        pltpu.VMEM((8, C), F32)],
        compiler_params=_cparams(("parallel", "arbitrary")),
        name="rg_scan_final" if final else "rg_scan",
    )(*args)


def _rope_tables(S, dh):
    rot = dh // ROT_FRAC
    half = rot // 2
    inv = ROPE_THETA ** (-jnp.arange(half, dtype=F32) * 2.0 / rot)
    ang = jnp.arange(S, dtype=F32)[:, None] * inv[None, :]
    cos, sin = jnp.cos(ang), jnp.sin(ang)
    rest = dh - rot
    c = jnp.concatenate([cos, cos, jnp.ones((S, rest), F32)], axis=1)
    s = jnp.concatenate([-sin, sin, jnp.zeros((S, rest), F32)], axis=1)
    reps = LANES // dh
    return _rope_layout(jnp.tile(c, (1, reps)), 1), _rope_layout(jnp.tile(s, (1, reps)), 1)


def _block_diag(w):
    n, c, e = w.shape
    eye = jnp.eye(n, dtype=w.dtype)
    return (w[:, :, None, :] * eye[:, None, :, None]).reshape(n * c, n * e)


def _prepare(p):
    depth, D = p['norm_mix'].shape
    F = p['ffn_w_down'].shape[1]
    fc = MXU_DIM
    nc = F // fc
    prep = {'norm_mix': p['norm_mix'].reshape(depth, 1, D), 'norm_ffn': p['norm_ffn'].reshape(depth, 1, D)}
    wup = p['ffn_w_up'].astype(BF16)
    prep['ffn_wg'] = wup[:, :, :F].reshape(depth, D, nc, fc).transpose(0, 2, 1, 3)
    prep['ffn_wu'] = wup[:, :, F:].reshape(depth, D, nc, fc).transpose(0, 2, 1, 3)
    cw = jnp.concatenate([p['ffn_conv_w'], p['ffn_conv_b'][:, None, :]], axis=1)
    cw = jnp.pad(cw, ((0, 0), (0, 8 - cw.shape[1]), (0, 0)))
    prep['ffn_cw'] = cw.reshape(depth, 8, nc, fc).transpose(0, 2, 1, 3)
    prep['ffn_wd'] = p['ffn_w_down'].astype(BF16).reshape(depth, nc, fc, D)

    n_da_q = DA_HEADS * 2
    prep['da_w_qkv'] = _rope_layout(p['da_w_qkv'], n_da_q).astype(BF16)
    da_geff = jnp.concatenate([jnp.tile(p['da_q_norm'] * (DA_DK ** -0.5 * LOG2E), (1, n_da_q)),
                               jnp.tile(p['da_k_norm'], (1, n_da_q))], axis=1)
    prep['da_geff'] = _rope_layout(da_geff, n_da_q)[:, None, :]
    prep['da_w_o'] = p['da_w_o'].astype(BF16)
    for name in ('da_lambda_q1', 'da_lambda_k1', 'da_lambda_q2', 'da_lambda_k2', 'da_sub_norm'):
        prep[name] = p[name][:, None, :]

    qw, kvw = WA_HEADS * WA_DH, WA_KV_HEADS * WA_DH
    w = p['wa_w_qkv']
    n_b = w.shape[0]

    def dup(cols):
        c = cols.reshape(n_b, D, WA_KV_HEADS, 1, WA_DH)
        return jnp.broadcast_to(c, (n_b, D, WA_KV_HEADS, 2, WA_DH)).reshape(n_b, D, 2 * kvw)

    n_wa_qk = (qw + 2 * kvw) // LANES
    wa_w = jnp.concatenate([w[:, :, :qw], dup(w[:, :, qw:qw + kvw]), dup(w[:, :, qw + kvw:])], axis=2)
    prep['wa_w_qkv'] = _rope_layout(wa_w, n_wa_qk).astype(BF16)
    wa_geff = jnp.concatenate([jnp.tile(p['wa_q_norm'] * (WA_DH ** -0.5 * LOG2E), (1, WA_HEADS)),
                               jnp.tile(p['wa_k_norm'], (1, 2 * WA_KV_HEADS))], axis=1)
    prep['wa_geff'] = _rope_layout(wa_geff, n_wa_qk)[:, None, :]
    prep['wa_sink'] = p['wa_sink'] * LOG2E
    prep['wa_w_o'] = p['wa_w_o'].astype(BF16)

    C = p['rg_w_out'].shape[1]
    bw = C // RG_BLOCKS
    win = p['rg_w_in'].astype(BF16)
    prep['rg_wg'], prep['rg_wu'] = win[:, :, :C], win[:, :, C:]
    rcw = jnp.concatenate([p['rg_conv_w'], p['rg_conv_b'][:, None, :]], axis=1)
    prep['rg_cw'] = jnp.pad(rcw, ((0, 0), (0, 8 - rcw.shape[1]), (0, 0)))
    gw, gb = p['rg_gate_w'], p['rg_gate_b']
    n_c = gw.shape[0]
    prep['rg_wr'] = jnp.stack([jnp.stack([_block_diag(gw[j, d, :, :, :bw]) for d in range(2)])
                               for j in range(n_c)]).astype(BF16)
    prep['rg_wi'] = jnp.stack([jnp.stack([_block_diag(gw[j, d, :, :, bw:]) for d in range(2)])
                               for j in range(n_c)]).astype(BF16)
    gvec = jnp.stack([gb[..., :bw].reshape(n_c, 2, C), gb[..., bw:].reshape(n_c, 2, C), p['rg_lambda']], axis=2)
    prep['rg_gb'] = jnp.pad(gvec, ((0, 0), (0, 0), (0, 5), (0, 0)))
    prep['rg_w_out'] = p['rg_w_out'].astype(BF16)
    return prep


def _lambda_init(layer_idx):
    return 0.8 - 0.6 * math.exp(-0.3 * layer_idx)


def _trunk(x, w):
    S = x.shape[1]
    cos, sin = _rope_tables(S, DA_DK)
    depth = w['norm_mix'].shape[0]
    for i in range(depth):
        kind, j = i % N_MIXERS, i // N_MIXERS
        gin = w['norm_mix'][i]
        if kind == 0:
            qkv = _project(x, gin, w['da_w_qkv'][j], w['da_geff'][j], cos, sin, 2 * DA_HEADS * LANES)
            o = _diff_attention(qkv, w['da_lambda_q1'][j], w['da_lambda_k1'][j], w['da_lambda_q2'][j],
                                w['da_lambda_k2'][j], w['da_sub_norm'][j], _lambda_init(i))
            x = _outproj(o, w['da_w_o'][j], x)
        elif kind == 1:
            n_prep = WA_HEADS * WA_DH + WA_KV_HEADS * LANES
            qkv = _project(x, gin, w['wa_w_qkv'][j], w['wa_geff'][j], cos, sin, n_prep)
            o = _window_attention(qkv, w['wa_sink'][j])
            x = _outproj(o, w['wa_w_o'][j], x)
        else:
            gate, u = _rg_in(x, gin, w['rg_wg'][j], w['rg_wu'][j], w['rg_cw'][j])
            hf = _rg_scan(u, w['rg_wr'][j, 0], w['rg_wi'][j, 0], w['rg_gb'][j, 0], reverse=False)
            x = _rg_scan(u, w['rg_wr'][j, 1], w['rg_wi'][j, 1], w['rg_gb'][j, 1], reverse=True,
                         extra=(hf, gate, x, w['rg_w_out'][j]))
        x = _conv_ffn(x, w['norm_ffn'][i], w['ffn_wg'][i], w['ffn_wu'][i], w['ffn_cw'][i], w['ffn_wd'][i])
    return x


def kernel(x_prompt, x_sample, norm_mix, norm_ffn, ffn_w_up, ffn_conv_w, ffn_conv_b, ffn_w_down, da_w_qkv, da_q_norm, da_k_norm, da_lambda_q1, da_lambda_k1, da_lambda_q2, da_lambda_k2, da_sub_norm, da_w_o, wa_w_qkv, wa_q_norm, wa_k_norm, wa_sink, wa_w_o, rg_w_in, rg_conv_w, rg_conv_b, rg_gate_w, rg_gate_b, rg_lambda, rg_w_out):
    w = _prepare(dict(
        norm_mix=norm_mix, norm_ffn=norm_ffn, ffn_w_up=ffn_w_up, ffn_conv_w=ffn_conv_w, ffn_conv_b=ffn_conv_b,
        ffn_w_down=ffn_w_down, da_w_qkv=da_w_qkv, da_q_norm=da_q_norm, da_k_norm=da_k_norm,
        da_lambda_q1=da_lambda_q1, da_lambda_k1=da_lambda_k1, da_lambda_q2=da_lambda_q2,
        da_lambda_k2=da_lambda_k2, da_sub_norm=da_sub_norm, da_w_o=da_w_o, wa_w_qkv=wa_w_qkv,
        wa_q_norm=wa_q_norm, wa_k_norm=wa_k_norm, wa_sink=wa_sink, wa_w_o=wa_w_o, rg_w_in=rg_w_in,
        rg_conv_w=rg_conv_w, rg_conv_b=rg_conv_b, rg_gate_w=rg_gate_w, rg_gate_b=rg_gate_b,
        rg_lambda=rg_lambda, rg_w_out=rg_w_out))
    return (_trunk(x_prompt, w), _trunk(x_sample, w))
```
